```python
import math
import jax
import jax.numpy as jnp
from jax import lax
import numpy as np

D_MODEL = 2048
BATCH = 4
SEQ = 2048
DEPTH = 2
DEC_BATCH = 128
DEC_SEQ = 4
PAST_LEN = 2048
PAGE_SIZE = 128

HEAD_DIM = 128
W_SB = 3 * D_MODEL // 8
H_SB = W_SB // HEAD_DIM
W_MB = 3 * D_MODEL // 8
H_MB = W_MB // HEAD_DIM
W_C = D_MODEL // 4
SSM_GROUP = 16
N_GROUPS = W_C // SSM_GROUP
SSM_STATE = 64
W_TOTAL = W_SB + W_MB + W_C
N_BRANCH = 3
IN_WIDTH = 4 * W_SB + 4 * W_MB + 2 * W_C + N_BRANCH * D_MODEL
SB_QBLOCK = 128
MOBA_BLOCK = 256
MOBA_TOPK = 3
MOBA_QUERY_ROWS = 128
ALPHA = (2 * DEPTH) ** 0.25
BETA = (8 * DEPTH) ** -0.25
LN_EPS = 1e-5

kernel_name = 'sb_moba_s5_gated_hybrid_step'


def _layer_norm(x, g, b):
    xf = x.astype(jnp.float32)
    mu = xf.mean(-1, keepdims=True)
    var = jnp.square(xf - mu).mean(-1, keepdims=True)
    y = (xf - mu) * lax.rsqrt(var + LN_EPS) * g.astype(jnp.float32) + b.astype(jnp.float32)
    return y.astype(x.dtype)


def _stick_breaking_attention(q, k, v):
    f32 = jnp.float32
    B, Lq, H, d = q.shape
    Lk = k.shape[1]
    off = Lk - Lq
    qb = math.gcd(Lq, SB_QBLOCK)
    nqb = Lq // qb
    qr = q.reshape(B, nqb, qb, H, d).swapaxes(0, 1)
    kf = k.astype(f32)
    vf = v.astype(f32)
    kpos = jnp.arange(Lk)
    scale = d ** -0.5

    def one_block(args):
        qblk, bi = args
        t = off + bi * qb + jnp.arange(qb)
        z = jnp.einsum('bqhd,bkhd->bhqk', qblk.astype(f32), kf) * scale
        mask = kpos[None, :] < t[:, None]
        log_keep = jnp.where(mask, jax.nn.log_sigmoid(-z), 0.0)
        later = lax.cumsum(log_keep, axis=3, reverse=True) - log_keep
        w = jnp.where(mask, jnp.exp(jax.nn.log_sigmoid(z) + later), 0.0)
        return jnp.einsum('bhqk,bkhd->bqhd', w, vf).astype(q.dtype)

    out = lax.map(one_block, (qr, jnp.arange(nqb)))
    return out.swapaxes(0, 1).reshape(B, Lq, H, d)


def _moba_attention(q, k, v):
    f32 = jnp.float32
    B, Lq, H, d = q.shape
    Lk = k.shape[1]
    off = Lk - Lq
    nb = max(-(-Lk // MOBA_BLOCK) + 1, MOBA_TOPK)
    pad = ((0, 0), (0, nb * MOBA_BLOCK - Lk), (0, 0), (0, 0))
    kbh = jnp.pad(k, pad).reshape(B, nb, MOBA_BLOCK, H, d).transpose(0, 3, 1, 2, 4)
    vbh = jnp.pad(v, pad).reshape(B, nb, MOBA_BLOCK, H, d).transpose(0, 3, 1, 2, 4)
    k_mean = kbh.astype(f32).mean(axis=3)
    slopes = jnp.exp2(-8.0 * jnp.arange(1, H + 1, dtype=f32) / H)[:, None, None]
    qb = math.gcd(Lq, max(1, MOBA_QUERY_ROWS // B))
    nqb = Lq // qb
    qr = q.reshape(B, nqb, qb, H, d).swapaxes(0, 1)
    blk_ids = jnp.arange(nb)
    r = jnp.arange(MOBA_BLOCK)
    n_sel = MOBA_TOPK * MOBA_BLOCK
    gather = jax.vmap(jax.vmap(lambda blocks, idx: blocks[idx]))
    scale = d ** -0.5

    def one_block(args):
        qblk, bi = args
        qf = qblk.astype(f32)
        p0 = off + bi * qb
        t = p0 + jnp.arange(qb)
        own = t // MOBA_BLOCK
        gate = jnp.einsum('bqhd,bhnd->bhqn', qf, k_mean)
        gate = jnp.where(blk_ids[None, :] < own[:, None], gate, -jnp.inf)
        gval, gidx = lax.top_k(gate, MOBA_TOPK)
        flat = gidx.reshape(B, H, qb * MOBA_TOPK)
        ks = gather(kbh, flat).reshape(B, H, qb, n_sel, d)
        vs = gather(vbh, flat).reshape(B, H, qb, n_sel, d)
        pos_sel = (gidx[..., None] * MOBA_BLOCK + r).reshape(B, H, qb, n_sel)
        ok_sel = jnp.repeat(jnp.isfinite(gval), MOBA_BLOCK, axis=-1)
        s_sel = jnp.einsum('bqhd,bhqkd->bhqk', qf, ks.astype(f32)) * scale - slopes * (t[:, None] - pos_sel)
        s_sel = jnp.where(ok_sel, s_sel, -jnp.inf)
        b0 = p0 // MOBA_BLOCK
        kl = lax.dynamic_slice_in_dim(kbh, b0, 2, axis=2).reshape(B, H, 2 * MOBA_BLOCK, d)
        vl = lax.dynamic_slice_in_dim(vbh, b0, 2, axis=2).reshape(B, H, 2 * MOBA_BLOCK, d)
        pos_loc = b0 * MOBA_BLOCK + jnp.arange(2 * MOBA_BLOCK)
        ok_loc = (pos_loc[None, :] <= t[:, None]) & (pos_loc[None, :] >= own[:, None] * MOBA_BLOCK)
        s_loc = jnp.einsum('bqhd,bhkd->bhqk', qf, kl.astype(f32)) * scale - slopes * (t[:, None] - pos_loc[None, :])
        s_loc = jnp.where(ok_loc, s_loc, -jnp.inf)
        p = jax.nn.softmax(jnp.concatenate([s_sel, s_loc], axis=-1), axis=-1)
        o = (jnp.einsum('bhqk,bhqkd->bqhd', p[..., :n_sel], vs.astype(f32))
             + jnp.einsum('bhqk,bhkd->bqhd', p[..., n_sel:], vl.astype(f32)))
        return o.astype(q.dtype)

    out = lax.map(one_block, (qr, jnp.arange(nqb)))
    return out.swapaxes(0, 1).reshape(B, Lq, H, d)


def _complex_affine_combine(e1, e2):
    a1r, a1i, b1r, b1i = e1
    a2r, a2i, b2r, b2i = e2
    return (a2r * a1r - a2i * a1i,
            a2r * a1i + a2i * a1r,
            a2r * b1r - a2i * b1i + b2r,
            a2r * b1i + a2i * b1r + b2i)


def _s5_branch(u, h0, a_re, a_im, log_step, b_re, b_im, c_re, c_im, d_skip, w_glu, b_glu):
    f32 = jnp.float32
    B, L, _ = u.shape
    ug = u.astype(f32).reshape(B, L, N_GROUPS, SSM_GROUP)
    a_re = a_re.astype(f32)
    a_im = a_im.astype(f32)
    step = jnp.exp(log_step.astype(f32))[:, None]
    mag = jnp.exp(a_re * step)
    ang = a_im * step
    abar_re = mag * jnp.cos(ang)
    abar_im = mag * jnp.sin(ang)
    den = a_re * a_re + a_im * a_im
    num_re = abar_re - 1.0
    coef_re = (num_re * a_re + abar_im * a_im) / den
    coef_im = (abar_im * a_re - num_re * a_im) / den
    br_ = b_re.astype(f32)
    bi_ = b_im.astype(f32)
    bbar_re = coef_re[..., None] * br_ - coef_im[..., None] * bi_
    bbar_im = coef_re[..., None] * bi_ + coef_im[..., None] * br_
    bu_re = jnp.einsum('blgh,gph->blgp', ug, bbar_re)
    bu_im = jnp.einsum('blgh,gph->blgp', ug, bbar_im)
    h0r = h0[..., 0].astype(f32)
    h0i = h0[..., 1].astype(f32)
    bu_re = bu_re.at[:, 0].add(abar_re * h0r - abar_im * h0i)
    bu_im = bu_im.at[:, 0].add(abar_re * h0i + abar_im * h0r)
    ar = jnp.broadcast_to(abar_re, bu_re.shape)
    ai = jnp.broadcast_to(abar_im, bu_im.shape)
    _, _, hr, hi = lax.associative_scan(_complex_affine_combine, (ar, ai, bu_re, bu_im), axis=1)
    y = (jnp.einsum('blgp,ghp->blgh', hr, c_re.astype(f32))
         - jnp.einsum('blgp,ghp->blgh', hi, c_im.astype(f32))
         + d_skip.astype(f32) * ug)
    y = jax.nn.gelu(y.reshape(B, L, W_C))
    y = y * jax.nn.sigmoid(y @ w_glu.astype(f32) + b_glu.astype(f32))
    h_last = jnp.stack([hr[:, -1], hi[:, -1]], axis=-1)
    return y.astype(u.dtype), h_last.astype(u.dtype)


def _trunk_layer(x, past_sb, past_mb, h0, w_in, b_gate, w_branch, w_out, ln_g, ln_b,
                 a_re, a_im, log_step, b_re, b_im, c_re, c_im, d_skip, w_glu, b_glu):
    B, L, _ = x.shape
    cuts = np.cumsum([W_SB] * 4 + [W_MB] * 4 + [W_C] * 2).tolist()
    (q_sb, k_sb, v_sb, g_sb, q_mb, k_mb, v_mb, g_mb, u_c, g_c,
     gate_logits) = jnp.split(x @ w_in, cuts, axis=-1)
    heads = lambda t: t.reshape(B, L, -1, HEAD_DIM)
    k_sb, v_sb, k_mb, v_mb = heads(k_sb), heads(v_sb), heads(k_mb), heads(v_mb)
    if past_sb is None:
        k_sb_all, v_sb_all, k_mb_all, v_mb_all = k_sb, v_sb, k_mb, v_mb
    else:
        k_sb_all = jnp.concatenate([past_sb[:, :, 0], k_sb], axis=1)
        v_sb_all = jnp.concatenate([past_sb[:, :, 1], v_sb], axis=1)
        k_mb_all = jnp.concatenate([past_mb[:, :, 0], k_mb], axis=1)
        v_mb_all = jnp.concatenate([past_mb[:, :, 1], v_mb], axis=1)
    o_sb = _stick_breaking_attention(heads(q_sb), k_sb_all, v_sb_all).reshape(B, L, W_SB) * jax.nn.silu(g_sb)
    o_mb = _moba_attention(heads(q_mb), k_mb_all, v_mb_all).reshape(B, L, W_MB) * jax.nn.silu(g_mb)
    o_c, h_last = _s5_branch(u_c, h0, a_re, a_im, log_step, b_re, b_im, c_re, c_im, d_skip, w_glu, b_glu)
    o_c = o_c * jax.nn.silu(g_c)
    gates = jax.nn.sigmoid(gate_logits.reshape(B, L, N_BRANCH, D_MODEL) + b_gate)
    merged = (gates[:, :, 0] * (o_sb @ w_branch[:W_SB])
              + gates[:, :, 1] * (o_mb @ w_branch[W_SB:W_SB + W_MB])
              + gates[:, :, 2] * (o_c @ w_branch[W_SB + W_MB:]))
    x_new = _layer_norm(ALPHA * x + merged @ w_out, ln_g, ln_b)
    return x_new, jnp.stack([k_sb, v_sb], axis=2), jnp.stack([k_mb, v_mb], axis=2), h_last


def setup_inputs(seed: int = 0) -> dict:
    key = jax.random.key(seed)
    ks = jax.random.split(key, 24)
    f32 = jnp.float32
    n_pages = PAST_LEN // PAGE_SIZE
    n_used = DEC_BATCH * n_pages
    n_pool = n_used + max(n_used // 4, 1)
    nrm = lambda k, shape, s=1.0: s * jax.random.normal(k, shape, f32)
    page_table = jax.random.permutation(ks[0], n_pool)[:n_used].reshape(DEC_BATCH, n_pages).astype(jnp.int32)
    row_scale = BETA * jnp.concatenate([jnp.full((W_SB,), W_SB ** -0.5, f32),
                                        jnp.full((W_MB,), W_MB ** -0.5, f32),
                                        jnp.full((W_C,), W_C ** -0.5, f32)])
    n_idx = jnp.arange(SSM_STATE, dtype=f32)
    return {
        'x_prompt': nrm(ks[1], (BATCH, SEQ, D_MODEL)),
        'x_sample': nrm(ks[2], (DEC_BATCH, DEC_SEQ, D_MODEL)),
        'cache_sb': nrm(ks[3], (DEPTH, n_pool, PAGE_SIZE, 2, H_SB, HEAD_DIM)),
        'cache_moba': nrm(ks[4], (DEPTH, n_pool, PAGE_SIZE, 2, H_MB, HEAD_DIM)),
        'state_ssm': nrm(ks[5], (DEPTH, DEC_BATCH, N_GROUPS, SSM_STATE, 2), 0.3),
        'page_table': page_table,
        'w_in': nrm(ks[6], (DEPTH, D_MODEL, IN_WIDTH), D_MODEL ** -0.5),
        'b_gate': nrm(ks[7], (DEPTH, N_BRANCH, D_MODEL), 0.02),
        'w_branch': nrm(ks[8], (DEPTH, W_TOTAL, D_MODEL)) * row_scale[:, None],
        'w_out': nrm(ks[9], (DEPTH, D_MODEL, D_MODEL), BETA * D_MODEL ** -0.5),
        'ln_g': 1.0 + nrm(ks[10], (DEPTH, D_MODEL), 0.02),
        'ln_b': nrm(ks[11], (DEPTH, D_MODEL), 0.02),
        'ssm_a_re': -0.5 + nrm(ks[12], (DEPTH, N_GROUPS, SSM_STATE), 0.01),
        'ssm_a_im': jnp.pi * n_idx + nrm(ks[13], (DEPTH, N_GROUPS, SSM_STATE), 0.01),
        'ssm_log_step': jax.random.uniform(ks[14], (DEPTH, N_GROUPS), f32, math.log(1e-3), math.log(1e-1)),
        'ssm_b_re': nrm(ks[15], (DEPTH, N_GROUPS, SSM_STATE, SSM_GROUP), (2 * SSM_GROUP) ** -0.5),
        'ssm_b_im': nrm(ks[16], (DEPTH, N_GROUPS, SSM_STATE, SSM_GROUP), (2 * SSM_GROUP) ** -0.5),
        'ssm_c_re': nrm(ks[17], (DEPTH, N_GROUPS, SSM_GROUP, SSM_STATE), SSM_STATE ** -0.5),
        'ssm_c_im': nrm(ks[18], (DEPTH, N_GROUPS, SSM_GROUP, SSM_STATE), SSM_STATE ** -0.5),
        'ssm_d': nrm(ks[19], (DEPTH, N_GROUPS, SSM_GROUP)),
        'ssm_w_glu': nrm(ks[20], (DEPTH, W_C, W_C), W_C ** -0.5),
        'ssm_b_glu': nrm(ks[21], (DEPTH, W_C), 0.02),
    }


def reference(x_prompt, x_sample, cache_sb, cache_moba, state_ssm, page_table, w_in, b_gate, w_branch,
              w_out, ln_g, ln_b, ssm_a_re, ssm_a_im, ssm_log_step, ssm_b_re, ssm_b_im, ssm_c_re, ssm_c_im,
              ssm_d, ssm_w_glu, ssm_b_glu):
    n_dec, n_pages = page_table.shape
    past_len = n_pages * cache_sb.shape[2]
    xp, xs = x_prompt, x_sample
    h0_prompt = jnp.zeros((x_prompt.shape[0], N_GROUPS, SSM_STATE, 2), x_prompt.dtype)
    sb_p, mb_p, ssm_p, sb_s, mb_s, ssm_s = [], [], [], [], [], []
    for l in range(DEPTH):
        lw = (w_in[l], b_gate[l], w_branch[l], w_out[l], ln_g[l], ln_b[l], ssm_a_re[l], ssm_a_im[l],
              ssm_log_step[l], ssm_b_re[l], ssm_b_im[l], ssm_c_re[l], ssm_c_im[l], ssm_d[l],
              ssm_w_glu[l], ssm_b_glu[l])
        xp, kv_sb, kv_mb, h_new = _trunk_layer(xp, None, None, h0_prompt, *lw)
        sb_p.append(kv_sb)
        mb_p.append(kv_mb)
        ssm_p.append(h_new)
        past_sb = cache_sb[l][page_table].reshape(n_dec, past_len, 2, H_SB, HEAD_DIM)
        past_mb = cache_moba[l][page_table].reshape(n_dec, past_len, 2, H_MB, HEAD_DIM)
        xs, kv_sb, kv_mb, h_new = _trunk_layer(xs, past_sb, past_mb, state_ssm[l], *lw)
        sb_s.append(kv_sb)
        mb_s.append(kv_mb)
        ssm_s.append(h_new)
    return (xp, xs, jnp.stack(sb_p), jnp.stack(mb_p), jnp.stack(ssm_p),
            jnp.stack(sb_s), jnp.stack(mb_s), jnp.stack(ssm_s))
```

```python
import functools

import jax
import jax.numpy as jnp
from jax import lax
from jax.experimental import pallas as pl
from jax.experimental.pallas import tpu as pltpu

F32 = jnp.float32
BF16 = jnp.bfloat16

HEAD_DIM = 128
MOBA_BLOCK = 256
MOBA_TOPK = 3
SSM_GROUP = 16
LN_EPS = 1e-5
NEG = -1e30

MIB = 1024 * 1024


def _nt(a, b):
    return lax.dot_general(a, b, (((1,), (1,)), ((), ())), preferred_element_type=F32)


def _mm(a, b):
    return jnp.dot(a, b, preferred_element_type=F32)


def _dot3(x, u):
    hi = x.astype(BF16)
    r1 = x - hi.astype(F32)
    mid = r1.astype(BF16)
    lo = (r1 - mid.astype(F32)).astype(BF16)
    return _mm(hi, u) + _mm(mid, u) + _mm(lo, u)


def _params(sem, vmem_mib):
    return pltpu.CompilerParams(dimension_semantics=sem, vmem_limit_bytes=vmem_mib * MIB)


def _proj_kernel(x_ref, w_ref, o_ref, wb_ref):
    @pl.when(pl.program_id(1) == 0)
    def _():
        wb_ref[...] = w_ref[...].astype(BF16)

    o_ref[...] = _mm(x_ref[...], wb_ref[...])


def _in_proj(xb, w_all, layer, tm=512, tn=1024):
    m, k = xb.shape
    n = w_all.shape[2]
    assert m % tm == 0 and n % tn == 0
    return pl.pallas_call(
        _proj_kernel,
        grid=(n // tn, m // tm),
        in_specs=[pl.BlockSpec((tm, k), lambda j, i: (i, 0)),
                  pl.BlockSpec((None, k, tn), lambda j, i: (layer, 0, j))],
        out_specs=pl.BlockSpec((tm, tn), lambda j, i: (i, j)),
        out_shape=jax.ShapeDtypeStruct((m, n), F32),
        scratch_shapes=[pltpu.VMEM((k, tn), BF16)],
        compiler_params=_params(("arbitrary", "arbitrary"), 44),
        name="in_proj",
    )(xb, w_all)


def _sb_block(z, v_bf, ustrict, carry, acc, valid):
    lg = jnp.log(1.0 + jnp.exp(-jnp.abs(z)))
    log_keep = -(jnp.maximum(z, 0.0) + lg)
    log_beta = jnp.minimum(z, 0.0) - lg
    if valid is not None:
        log_keep = jnp.where(valid, log_keep, 0.0)
    later = _dot3(log_keep, ustrict) + carry
    w = jnp.exp(log_beta + later)
    if valid is not None:
        w = jnp.where(valid, w, 0.0)
    acc = acc + _mm(w.astype(BF16), v_bf)
    carry = carry + jnp.sum(log_keep, axis=1, keepdims=True)
    return carry, acc


def _strict_lower(n):
    r = lax.broadcasted_iota(jnp.int32, (n, n), 0)
    c = lax.broadcasted_iota(jnp.int32, (n, n), 1)
    return jnp.where(r > c, 1.0, 0.0).astype(BF16)


def _sb_prompt_kernel(q_ref, k_ref, v_ref, o_ref, *, tq, scale):
    qi = pl.program_id(2)
    q = q_ref[...].astype(BF16)
    ustrict = _strict_lower(tq)
    row = lax.broadcasted_iota(jnp.int32, (tq, tq), 0)
    col = lax.broadcasted_iota(jnp.int32, (tq, tq), 1)

    def block(kb, carry, acc, valid):
        start = pl.multiple_of(kb * tq, tq)
        k = k_ref[pl.ds(start, tq), :].astype(BF16)
        v = v_ref[pl.ds(start, tq), :].astype(BF16)
        z = _nt(q, k) * scale
        return _sb_block(z, v, ustrict, carry, acc, valid)

    carry = jnp.zeros((tq, 1), F32)
    acc = jnp.zeros((tq, HEAD_DIM), F32)
    carry, acc = block(qi, carry, acc, col < row)

    def body(it, c):
        return block(qi - 1 - it, c[0], c[1], None)

    carry, acc = lax.fori_loop(0, qi, body, (carry, acc))
    o_ref[...] = acc


def _sb_prompt(p, batch, seq, heads, q_c0, k_c0, v_c0, tq=256):
    assert seq % tq == 0
    nq = seq // tq
    return pl.pallas_call(
        functools.partial(_sb_prompt_kernel, tq=tq, scale=HEAD_DIM ** -0.5),
        grid=(batch, heads, nq),
        in_specs=[pl.BlockSpec((tq, HEAD_DIM), lambda b, h, i: (b * nq + i, q_c0 + h)),
                  pl.BlockSpec((seq, HEAD_DIM), lambda b, h, i: (b, k_c0 + h)),
                  pl.BlockSpec((seq, HEAD_DIM), lambda b, h, i: (b, v_c0 + h))],
        out_specs=pl.BlockSpec((tq, HEAD_DIM), lambda b, h, i: (b * nq + i, h)),
        out_shape=jax.ShapeDtypeStruct((batch * seq, heads * HEAD_DIM), F32),
        compiler_params=_params(("arbitrary", "arbitrary", "arbitrary"), 32),
        name="sb_prompt",
    )(p, p, p)


def _sb_sample_kernel(pt_ref, qbd_ref, kvn_ref, *refs, n_pages, page, width, dec, scale):
    del pt_ref
    page_refs, o_ref = refs[:n_pages], refs[n_pages]
    rows = qbd_ref.shape[0]
    qbd = qbd_ref[...].astype(BF16)
    ustrict = _strict_lower(page)
    i_row = lax.broadcasted_iota(jnp.int32, (rows, page), 0) % dec
    lane = lax.broadcasted_iota(jnp.int32, (rows, page), 1)

    kvn = kvn_ref[...]
    pad = jnp.zeros((page - kvn.shape[0], 2 * width), F32)
    kvn = jnp.concatenate([kvn, pad], axis=0).astype(BF16)
    carry = jnp.zeros((rows, 1), F32)
    acc = jnp.zeros((rows, width), F32)
    z = _nt(qbd, kvn[:, :width]) * scale
    carry, acc = _sb_block(z, kvn[:, width:], ustrict, carry, acc, lane < i_row)
    for pg in reversed(range(n_pages)):
        kv = page_refs[pg][...].astype(BF16)
        z = _nt(qbd, kv[:, :width]) * scale
        carry, acc = _sb_block(z, kv[:, width:], ustrict, carry, acc, None)
    o_ref[...] = acc


def _page_specs(layer, n_pages, page, width2):
    def spec(pg):
        return pl.BlockSpec((None, None, page, width2),
                            lambda n, pt: (layer, pt[n * n_pages + pg], 0, 0))
    return [spec(pg) for pg in range(n_pages)]


def _sb_sample(qbd, kv_new, cache, page_table, layer, dec):
    n, rows, width = qbd.shape
    n_pages = page_table.shape[1]
    page = cache.shape[2]
    kern = functools.partial(_sb_sample_kernel, n_pages=n_pages, page=page, width=width, dec=dec,
                             scale=HEAD_DIM ** -0.5)
    grid_spec = pltpu.PrefetchScalarGridSpec(
        num_scalar_prefetch=1,
        grid=(n,),
        in_specs=[pl.BlockSpec((None, rows, width), lambda i, pt: (i, 0, 0)),
                  pl.BlockSpec((None, kv_new.shape[1], 2 * width), lambda i, pt: (i, 0, 0))]
                 + _page_specs(layer, n_pages, page, 2 * width),
        out_specs=pl.BlockSpec((None, rows, width), lambda i, pt: (i, 0, 0)),
    )
    return pl.pallas_call(
        kern, grid_spec=grid_spec,
        out_shape=jax.ShapeDtypeStruct((n, rows, width), F32),
        compiler_params=_params(("arbitrary",), 48),
        name="sb_sample",
    )(page_table.reshape(-1), qbd, kv_new, *([cache] * n_pages))


def _bf_split(x):
    hi = x.astype(BF16)
    lo = (x - hi.astype(F32)).astype(BF16)
    return hi, lo


def _topk_bias_t(gate_t, own):
    nb = gate_t.shape[0]
    nidx = lax.broadcasted_iota(jnp.int32, gate_t.shape, 0)
    rank = jnp.zeros(gate_t.shape, F32)
    for mblk in range(nb):
        gm = gate_t[mblk:mblk + 1, :]
        beats = jnp.where(gm > gate_t, 1.0, jnp.where(gm == gate_t, jnp.where(nidx > mblk, 1.0, 0.0), 0.0))
        rank = rank + jnp.where(mblk < own, beats, 0.0)
    sel = jnp.where(nidx < own, jnp.where(rank < MOBA_TOPK, 1.0, 0.0), 0.0)
    return jnp.where(sel > 0.0, 0.0, NEG)


def _moba_prompt_kernel(slope_ref, q_ref, k_ref, v_ref, o_ref, kmean_ref, *, tq, nblk, scale):
    h = pl.program_id(1)
    qi = pl.program_id(2)
    slope = slope_ref[h]

    @pl.when(qi == 0)
    def _():
        kmean_ref[...] = jnp.zeros(kmean_ref.shape, F32)
        for nb in range(nblk):
            kb = k_ref[nb * tq:(nb + 1) * tq, :]
            kmean_ref[nb:nb + 1, :] = jnp.sum(kb, axis=0, keepdims=True) * (1.0 / tq)

    qf = q_ref[...]
    q = qf.astype(BF16)
    q_hi, q_lo = _bf_split(qf)
    m_hi, m_lo = _bf_split(kmean_ref[...])
    gate_t = _nt(m_hi, q_hi) + _nt(m_hi, q_lo) + _nt(m_lo, q_hi)
    bias_t = _topk_bias_t(gate_t, qi)
    bias_t = jnp.concatenate([bias_t, jnp.zeros((HEAD_DIM - bias_t.shape[0], tq), F32)], axis=0)
    r = lax.broadcasted_iota(jnp.int32, (tq, tq), 0)
    c = lax.broadcasted_iota(jnp.int32, (tq, tq), 1)
    eye = jnp.where(r == c, 1.0, 0.0).astype(BF16)
    bias_q = _nt(eye, bias_t.astype(BF16))
    qa = jnp.concatenate([q, bias_q.astype(BF16)], axis=1)
    rel = (r - c).astype(F32)

    start = pl.multiple_of(qi * tq, tq)
    k = k_ref[pl.ds(start, tq), :].astype(BF16)
    v = v_ref[pl.ds(start, tq), :].astype(BF16)
    s = _nt(q, k) * scale - slope * rel
    s = jnp.where(c <= r, s, NEG)
    m = jnp.max(s, axis=1, keepdims=True)
    p = jnp.exp(s - m)
    l = jnp.sum(p, axis=1, keepdims=True)
    acc = _mm(p.astype(BF16), v)
    lane = lax.broadcasted_iota(jnp.int32, (tq, HEAD_DIM), 1)

    def body(kb, carry):
        m, l, acc = carry
        start = pl.multiple_of(kb * tq, tq)
        k = k_ref[pl.ds(start, tq), :].astype(BF16)
        v = v_ref[pl.ds(start, tq), :].astype(BF16)
        onehot = jnp.where(lane == kb, 1.0, 0.0).astype(BF16)
        s = _nt(qa, jnp.concatenate([k, onehot], axis=1))
        dist = rel + ((qi - kb) * tq).astype(F32)
        s = s * scale - slope * dist
        m_new = jnp.maximum(m, jnp.max(s, axis=1, keepdims=True))
        alpha = jnp.exp(m - m_new)
        p = jnp.exp(s - m_new)
        l = alpha * l + jnp.sum(p, axis=1, keepdims=True)
        acc = alpha * acc + _mm(p.astype(BF16), v)
        return m_new, l, acc

    m, l, acc = lax.fori_loop(0, qi, body, (m, l, acc))
    o_ref[...] = acc / l


def _alibi_slopes(heads):
    return jnp.exp2(-8.0 * jnp.arange(1, heads + 1, dtype=F32) / heads)


def _moba_prompt(p, batch, seq, heads, q_c0, k_c0, v_c0):
    tq = MOBA_BLOCK
    assert seq % tq == 0 and seq // tq <= 8
    nq = seq // tq
    kern = functools.partial(_moba_prompt_kernel, tq=tq, nblk=nq, scale=HEAD_DIM ** -0.5)
    return pl.pallas_call(
        kern,
        grid=(batch, heads, nq),
        in_specs=[pl.BlockSpec(memory_space=pltpu.SMEM),
                  pl.BlockSpec((tq, HEAD_DIM), lambda b, h, i: (b * nq + i, q_c0 + h)),
                  pl.BlockSpec((seq, HEAD_DIM), lambda b, h, i: (b, k_c0 + h)),
                  pl.BlockSpec((seq, HEAD_DIM), lambda b, h, i: (b, v_c0 + h))],
        out_specs=pl.BlockSpec((tq, HEAD_DIM), lambda b, h, i: (b * nq + i, h)),
        out_shape=jax.ShapeDtypeStruct((batch * seq, heads * HEAD_DIM), F32),
        scratch_shapes=[pltpu.VMEM((8, HEAD_DIM), F32)],
        compiler_params=_params(("arbitrary", "arbitrary", "arbitrary"), 32),
        name="moba_prompt",
    )(_alibi_slopes(heads), p, p, p)


def _moba_sample_kernel(pt_ref, qbd_ref, kvn_ref, slope_ref, *refs, n_pages, page, width, dec, scale):
    del pt_ref
    page_refs, o_ref = refs[:n_pages], refs[n_pages]
    rows = qbd_ref.shape[0]
    per_blk = MOBA_BLOCK // page
    nblk = n_pages // per_blk
    past_len = n_pages * page
    qbd = qbd_ref[...].astype(BF16)
    slope = slope_ref[...]
    i_row = lax.broadcasted_iota(jnp.int32, (rows, page), 0) % dec
    lane = lax.broadcasted_iota(jnp.int32, (rows, page), 1)

    raw = [_nt(qbd, page_refs[pg][:, :width].astype(BF16)) for pg in range(n_pages)]
    gates = []
    for nb in range(nblk):
        tot = raw[nb * per_blk]
        for j in range(1, per_blk):
            tot = tot + raw[nb * per_blk + j]
        gates.append(jnp.sum(tot, axis=1, keepdims=True) * (1.0 / MOBA_BLOCK))
    sel_bias = []
    for nb in range(nblk):
        rank = jnp.zeros((rows, 1), F32)
        for mb in range(nblk):
            if mb == nb:
                continue
            beats = (gates[mb] >= gates[nb]) if mb < nb else (gates[mb] > gates[nb])
            rank = rank + jnp.where(beats, 1.0, 0.0)
        sel_bias.append(jnp.where(rank < MOBA_TOPK, 0.0, NEG))

    t_pos = (past_len + i_row).astype(F32)
    scores = []
    for pg in range(n_pages):
        dist = t_pos - (pg * page + lane).astype(F32)
        scores.append(raw[pg] * scale - slope * dist + sel_bias[pg // per_blk])
    kvn = kvn_ref[...]
    pad = jnp.zeros((page - kvn.shape[0], 2 * width), F32)
    kvn = jnp.concatenate([kvn, pad], axis=0).astype(BF16)
    s_new = _nt(qbd, kvn[:, :width]) * scale - slope * (i_row - lane).astype(F32)
    s_new = jnp.where(lane <= i_row, s_new, NEG)

    m = jnp.max(s_new, axis=1, keepdims=True)
    for s in scores:
        m = jnp.maximum(m, jnp.max(s, axis=1, keepdims=True))
    p_new = jnp.exp(s_new - m)
    l = jnp.sum(p_new, axis=1, keepdims=True)
    acc = _mm(p_new.astype(BF16), kvn[:, width:])
    for pg in range(n_pages):
        p = jnp.exp(scores[pg] - m)
        l = l + jnp.sum(p, axis=1, keepdims=True)
        acc = acc + _mm(p.astype(BF16), page_refs[pg][:, width:].astype(BF16))
    o_ref[...] = acc / l


def _moba_sample(qbd, kv_new, slope_rows, cache, page_table, layer, dec):
    n, rows, width = qbd.shape
    n_pages = page_table.shape[1]
    page = cache.shape[2]
    assert MOBA_BLOCK % page == 0 and (n_pages * page) % MOBA_BLOCK == 0 and dec <= MOBA_BLOCK
    kern = functools.partial(_moba_sample_kernel, n_pages=n_pages, page=page, width=width, dec=dec,
                             scale=HEAD_DIM ** -0.5)
    grid_spec = pltpu.PrefetchScalarGridSpec(
        num_scalar_prefetch=1,
        grid=(n,),
        in_specs=[pl.BlockSpec((None, rows, width), lambda i, pt: (i, 0, 0)),
                  pl.BlockSpec((None, kv_new.shape[1], 2 * width), lambda i, pt: (i, 0, 0)),
                  pl.BlockSpec((rows, 1), lambda i, pt: (0, 0))]
                 + _page_specs(layer, n_pages, page, 2 * width),
        out_specs=pl.BlockSpec((None, rows, width), lambda i, pt: (i, 0, 0)),
    )
    return pl.pallas_call(
        kern, grid_spec=grid_spec,
        out_shape=jax.ShapeDtypeStruct((n, rows, width), F32),
        compiler_params=_params(("arbitrary",), 48),
        name="moba_sample",
    )(page_table.reshape(-1), qbd, kv_new, slope_rows, *([cache] * n_pages))


def _ssm_disc_kernel(are_ref, aim_ref, ls_ref, bre_ref, bim_ref, abr_ref, abi_ref, bbr_ref, bbi_ref):
    a_re = are_ref[...]
    a_im = aim_ref[...]
    step = jnp.exp(ls_ref[...])
    mag = jnp.exp(a_re * step)
    ang = a_im * step
    abar_re = mag * jnp.cos(ang)
    abar_im = mag * jnp.sin(ang)
    den = a_re * a_re + a_im * a_im
    num_re = abar_re - 1.0
    coef_re = (num_re * a_re + abar_im * a_im) / den
    coef_im = (abar_im * a_re - num_re * a_im) / den
    b_re = bre_ref[...]
    b_im = bim_ref[...]
    abr_ref[...] = abar_re
    abi_ref[...] = abar_im
    bbr_ref[...] = coef_re * b_re - coef_im * b_im
    bbi_ref[...] = coef_re * b_im + coef_im * b_re


def _ssm_discretise(a_re, a_im, log_step, b_re, b_im):
    g, pdim = a_re.shape
    hdim = b_re.shape[2]
    rep = lambda a: jnp.repeat(a, hdim, axis=0)
    ls = jnp.broadcast_to(log_step[:, None], (g, pdim))
    bt = lambda b: b.transpose(0, 2, 1).reshape(g * hdim, pdim)
    shp = jax.ShapeDtypeStruct((g * hdim, pdim), F32)
    abr, abi, bbr, bbi = pl.pallas_call(
        _ssm_disc_kernel, out_shape=(shp, shp, shp, shp), name="ssm_disc",
    )(rep(a_re), rep(a_im), rep(ls), bt(b_re), bt(b_im))
    abar = jnp.stack([abr[::hdim].reshape(-1), abi[::hdim].reshape(-1)])
    blockdiag = lambda m: (jnp.eye(g, dtype=F32)[:, None, :, None]
                           * m.reshape(g, hdim, 1, pdim)).reshape(g * hdim, g * pdim)
    bblk = jnp.concatenate([blockdiag(bbr), blockdiag(bbi)], axis=1)
    return abar, bblk


def _ssm_scan_kernel(u_ref, h0_ref, abar_ref, bblk_ref, cre_ref, cim_ref, d_ref, wglu_ref, bglu_ref,
                     o_ref, hlast_ref, h_ref, bu_ref, *, rows, steps, nstate):
    @pl.when(pl.program_id(0) == 0)
    def _():
        h_ref[...] = h0_ref[...]

    u = u_ref[...]
    bu_ref[...] = _mm(u.astype(BF16), bblk_ref[...])
    a_re = abar_ref[0:1, :]
    a_im = abar_ref[1:2, :]

    def advance(h_re, h_im, x_re, x_im):
        return a_re * h_re - a_im * h_im + x_re, a_re * h_im + a_im * h_re + x_im

    if rows % 8 == 0:
        def step(t, carry):
            r0 = pl.multiple_of(t * rows, rows)
            n_re, n_im = advance(h_ref[:, :nstate], h_ref[:, nstate:],
                                 bu_ref[pl.ds(r0, rows), :nstate], bu_ref[pl.ds(r0, rows), nstate:])
            h_ref[:, :nstate] = n_re
            h_ref[:, nstate:] = n_im
            bu_ref[pl.ds(r0, rows), :nstate] = n_re
            bu_ref[pl.ds(r0, rows), nstate:] = n_im
            return carry

        lax.fori_loop(0, steps, step, 0)
    else:
        assert rows == 4 and steps % 2 == 0
        top = lax.broadcasted_iota(jnp.int32, (8, nstate), 0) < 4

        def step2(t, carry):
            r0 = pl.multiple_of(t * 8, 8)
            x_re = bu_ref[pl.ds(r0, 8), :nstate]
            x_im = bu_ref[pl.ds(r0, 8), nstate:]
            e_re, e_im = advance(pltpu.roll(h_ref[:, :nstate], 4, axis=0),
                                 pltpu.roll(h_ref[:, nstate:], 4, axis=0), x_re, x_im)
            o_re, o_im = advance(pltpu.roll(e_re, 4, axis=0), pltpu.roll(e_im, 4, axis=0), x_re, x_im)
            n_re = jnp.where(top, e_re, o_re)
            n_im = jnp.where(top, e_im, o_im)
            h_ref[:, :nstate] = n_re
            h_ref[:, nstate:] = n_im
            bu_ref[pl.ds(r0, 8), :nstate] = n_re
            bu_ref[pl.ds(r0, 8), nstate:] = n_im
            return carry

        lax.fori_loop(0, steps // 2, step2, 0)
    hlast_ref[...] = h_ref[...]
    y = (_mm(bu_ref[:, :nstate].astype(BF16), cre_ref[...])
         - _mm(bu_ref[:, nstate:].astype(BF16), cim_ref[...]) + d_ref[...] * u)
    y = jax.nn.gelu(y)
    o_ref[...] = y * jax.nn.sigmoid(_mm(y.astype(BF16), wglu_ref[...]) + bglu_ref[...])


def _ssm_scan(u, h0, abar, bblk, cre, cim, d, wglu, bglu, rows, steps_per_chunk):
    total, wc = u.shape
    nstate = abar.shape[1]
    chunk = rows * steps_per_chunk
    assert total % chunk == 0
    srows = max(rows, 8)
    h0 = jnp.pad(h0, ((srows - rows, 0), (0, 0)))
    const = lambda shape: pl.BlockSpec(shape, lambda c: (0,) * len(shape))
    kern = functools.partial(_ssm_scan_kernel, rows=rows, steps=steps_per_chunk, nstate=nstate)
    y, h_last = pl.pallas_call(
        kern,
        grid=(total // chunk,),
        in_specs=[pl.BlockSpec((chunk, wc), lambda c: (c, 0)),
                  const((srows, 2 * nstate)), const((2, nstate)), const((wc, 2 * nstate)),
                  const((nstate, wc)), const((nstate, wc)), const((1, wc)), const((wc, wc)),
                  const((1, wc))],
        out_specs=[pl.BlockSpec((chunk, wc), lambda c: (c, 0)), const((srows, 2 * nstate))],
        out_shape=[jax.ShapeDtypeStruct((total, wc), F32),
                   jax.ShapeDtypeStruct((srows, 2 * nstate), F32)],
        scratch_shapes=[pltpu.VMEM((srows, 2 * nstate), F32), pltpu.VMEM((chunk, 2 * nstate), F32)],
        compiler_params=_params(("arbitrary",), 48),
        name="ssm_scan",
    )(u, h0, abar, bblk, cre, cim, d, wglu, bglu)
    return y, h_last[srows - rows:]


def _post_kernel(x_ref, osb_ref, omb_ref, oc_ref, gsb_ref, gmb_ref, gc_ref,
                 gl0, gl1, gl2, gl3, gl4, gl5, bg_ref, wb_ref, wo_ref, lng_ref, lnb_ref,
                 y_ref, yb_ref, *, w_sb, w_mb, alpha):
    dm = wo_ref.shape[0]
    half = dm // 2
    gls = ((gl0, gl1), (gl2, gl3), (gl4, gl5))
    acts = (osb_ref[...] * jax.nn.silu(gsb_ref[...]),
            omb_ref[...] * jax.nn.silu(gmb_ref[...]),
            oc_ref[...] * jax.nn.silu(gc_ref[...]))
    cuts = (0, w_sb, w_sb + w_mb, wb_ref.shape[0])
    merged = None
    for i in range(3):
        proj = _mm(acts[i].astype(BF16), wb_ref[cuts[i]:cuts[i + 1], :])
        logits = jnp.concatenate([gls[i][0][...], gls[i][1][...]], axis=1) + bg_ref[i:i + 1, :]
        term = jax.nn.sigmoid(logits) * proj
        merged = term if merged is None else merged + term
    del half
    yv = alpha * x_ref[...] + _mm(merged.astype(BF16), wo_ref[...])
    mu = jnp.mean(yv, axis=1, keepdims=True)
    cen = yv - mu
    var = jnp.mean(cen * cen, axis=1, keepdims=True)
    out = cen * lax.rsqrt(var + LN_EPS) * lng_ref[...] + lnb_ref[...]
    y_ref[...] = out
    yb_ref[...] = out.astype(BF16)


def _post(x, p, o_sb, o_mb, o_c, b_gate, wb, wo, ln_g, ln_b, cols, alpha, tm=256):
    m, dm = x.shape
    w_sb, w_mb, w_c = o_sb.shape[1], o_mb.shape[1], o_c.shape[1]
    half = dm // 2
    assert m % tm == 0 and cols["g_sb"] % w_sb == 0 and cols["g_mb"] % w_mb == 0
    assert cols["g_c"] % w_c == 0 and cols["gate"] % half == 0
    row = lambda width, blk: pl.BlockSpec((tm, width), lambda i: (i, blk))
    const = lambda shape: pl.BlockSpec(shape, lambda i: (0,) * len(shape), pipeline_mode=pl.Buffered(1))
    gl_specs = [row(half, cols["gate"] // half + j) for j in range(6)]
    kern = functools.partial(_post_kernel, w_sb=w_sb, w_mb=w_mb, alpha=alpha)
    return pl.pallas_call(
        kern,
        grid=(m // tm,),
        in_specs=[row(dm, 0), row(w_sb, 0), row(w_mb, 0), row(w_c, 0),
                  row(w_sb, cols["g_sb"] // w_sb), row(w_mb, cols["g_mb"] // w_mb),
                  row(w_c, cols["g_c"] // w_c)] + gl_specs
                 + [const((3, dm)), const(wb.shape), const(wo.shape), const((1, dm)), const((1, dm))],
        out_specs=[row(dm, 0), row(dm, 0)],
        out_shape=[jax.ShapeDtypeStruct((m, dm), F32), jax.ShapeDtypeStruct((m, dm), BF16)],
        compiler_params=_params(("arbitrary",), 56),
        name="post",
    )(x, o_sb, o_mb, o_c, p, p, p, *([p] * 6), b_gate, wb, wo, ln_g, ln_b)


def _block_diag_queries(q, heads):
    n, dec, width = q.shape
    qh = q.reshape(n, dec, heads, HEAD_DIM)
    eye = jnp.eye(heads, dtype=q.dtype)
    out = jnp.einsum("nihd,hg->nhigd", qh, eye)
    return out.reshape(n, heads * dec, width)


def _diag_heads(o, heads, dec):
    n = o.shape[0]
    o5 = o.reshape(n, heads, dec, heads, HEAD_DIM)
    d = jnp.stack([o5[:, h, :, h, :] for h in range(heads)], axis=2)
    return d.reshape(n * dec, heads * HEAD_DIM)


def kernel(x_prompt, x_sample, cache_sb, cache_moba, state_ssm, page_table, w_in, b_gate, w_branch,
           w_out, ln_g, ln_b, ssm_a_re, ssm_a_im, ssm_log_step, ssm_b_re, ssm_b_im, ssm_c_re, ssm_c_im,
           ssm_d, ssm_w_glu, ssm_b_glu):
    batch, seq, dm = x_prompt.shape
    n_dec, dec, _ = x_sample.shape
    depth = w_in.shape[0]
    h_sb = cache_sb.shape[4]
    h_mb = cache_moba.shape[4]
    w_sb, w_mb = h_sb * HEAD_DIM, h_mb * HEAD_DIM
    groups, nst = ssm_a_re.shape[1], ssm_a_re.shape[2]
    w_c = groups * SSM_GROUP
    n_p, n_s = batch * seq, n_dec * dec
    alpha = (2 * depth) ** 0.25
    page = cache_sb.shape[2]

    c_sb, c_mb = 0, 4 * w_sb
    c_u = c_mb + 4 * w_mb
    cols = {"g_sb": c_sb + 3 * w_sb, "g_mb": c_mb + 3 * w_mb, "g_c": c_u + w_c, "gate": c_u + 2 * w_c}
    blk = lambda col: col // HEAD_DIM

    x = jnp.concatenate([x_prompt.reshape(n_p, dm), x_sample.reshape(n_s, dm)], axis=0)
    xb = x.astype(BF16)
    cache_sb2 = cache_sb.reshape(depth, cache_sb.shape[1], page, 2 * w_sb)
    cache_mb2 = cache_moba.reshape(depth, cache_moba.shape[1], page, 2 * w_mb)
    slope_rows = jnp.repeat(_alibi_slopes(h_mb), dec)[:, None]
    wb_bf = w_branch.astype(BF16)
    wo_bf = w_out.astype(BF16)
    wglu_bf = ssm_w_glu.astype(BF16)
    eye_g = jnp.eye(groups, dtype=F32)

    outs = {k: [] for k in ("sb_p", "mb_p", "ssm_p", "sb_s", "mb_s", "ssm_s")}
    for l in range(depth):
        p = _in_proj(xb, w_in, l)
        kv_sb = p[:, c_sb + w_sb:c_sb + 3 * w_sb]
        kv_mb = p[:, c_mb + w_mb:c_mb + 3 * w_mb]
        outs["sb_p"].append(kv_sb[:n_p].reshape(batch, seq, 2, h_sb, HEAD_DIM))
        outs["mb_p"].append(kv_mb[:n_p].reshape(batch, seq, 2, h_mb, HEAD_DIM))
        outs["sb_s"].append(kv_sb[n_p:].reshape(n_dec, dec, 2, h_sb, HEAD_DIM))
        outs["mb_s"].append(kv_mb[n_p:].reshape(n_dec, dec, 2, h_mb, HEAD_DIM))

        o_sb_p = _sb_prompt(p, batch, seq, h_sb, blk(c_sb), blk(c_sb + w_sb), blk(c_sb + 2 * w_sb))
        o_mb_p = _moba_prompt(p, batch, seq, h_mb, blk(c_mb), blk(c_mb + w_mb), blk(c_mb + 2 * w_mb))

        pad_new = lambda kv: jnp.pad(kv[n_p:].reshape(n_dec, dec, -1), ((0, 0), (0, 8 - dec), (0, 0)))
        q_sb_s = p[n_p:, c_sb:c_sb + w_sb].reshape(n_dec, dec, w_sb)
        q_mb_s = p[n_p:, c_mb:c_mb + w_mb].reshape(n_dec, dec, w_mb)
        o_sb_s = _sb_sample(_block_diag_queries(q_sb_s, h_sb), pad_new(kv_sb), cache_sb2, page_table, l, dec)
        o_mb_s = _moba_sample(_block_diag_queries(q_mb_s, h_mb), pad_new(kv_mb), slope_rows, cache_mb2,
                              page_table, l, dec)
        o_sb = jnp.concatenate([o_sb_p, _diag_heads(o_sb_s, h_sb, dec)], axis=0)
        o_mb = jnp.concatenate([o_mb_p, _diag_heads(o_mb_s, h_mb, dec)], axis=0)

        abar, bblk = _ssm_discretise(ssm_a_re[l], ssm_a_im[l], ssm_log_step[l], ssm_b_re[l], ssm_b_im[l])
        cblk = lambda cm: (eye_g[:, None, :, None] * cm.transpose(0, 2, 1)[:, :, None, :]
                           ).reshape(groups * nst, w_c).astype(BF16)
        ssm_w = (abar, bblk.astype(BF16), cblk(ssm_c_re[l]), cblk(ssm_c_im[l]), ssm_d[l].reshape(1, w_c),
                 wglu_bf[l], ssm_b_glu[l].reshape(1, w_c))
        u = p[:, c_u:c_u + w_c]
        u_p = u[:n_p].reshape(batch, seq, w_c).transpose(1, 0, 2).reshape(n_p, w_c)
        u_s = u[n_p:].reshape(n_dec, dec, w_c).transpose(1, 0, 2).reshape(n_s, w_c)
        y_p, hl_p = _ssm_scan(u_p, jnp.zeros((batch, 2 * groups * nst), F32), *ssm_w,
                              rows=batch, steps_per_chunk=128)
        st = state_ssm[l].reshape(n_dec, groups * nst, 2)
        h0_s = jnp.concatenate([st[..., 0], st[..., 1]], axis=1)
        y_s, hl_s = _ssm_scan(u_s, h0_s, *ssm_w, rows=n_dec, steps_per_chunk=dec)
        o_c = jnp.concatenate([y_p.reshape(seq, batch, w_c).transpose(1, 0, 2).reshape(n_p, w_c),
                               y_s.reshape(dec, n_dec, w_c).transpose(1, 0, 2).reshape(n_s, w_c)], axis=0)
        unstate = lambda hl: jnp.stack([hl[:, :groups * nst], hl[:, groups * nst:]], axis=-1
                                       ).reshape(hl.shape[0], groups, nst, 2)
        outs["ssm_p"].append(unstate(hl_p))
        outs["ssm_s"].append(unstate(hl_s))

        x, xb = _post(x, p, o_sb, o_mb, o_c, b_gate[l], wb_bf[l], wo_bf[l], ln_g[l].reshape(1, dm),
                      ln_b[l].reshape(1, dm), cols, alpha)

    return (x[:n_p].reshape(batch, seq, dm), x[n_p:].reshape(n_dec, dec, dm),
            jnp.stack(outs["sb_p"]), jnp.stack(outs["mb_p"]), jnp.stack(outs["ssm_p"]),
            jnp.stack(outs["sb_s"]), jnp.stack(outs["mb_s"]), jnp.stack(outs["ssm_s"]))
```

```python
import functools

import jax
import jax.numpy as jnp
from jax import lax
from jax.experimental import pallas as pl
from jax.experimental.pallas import tpu as pltpu

F32 = jnp.float32
BF16 = jnp.bfloat16

HEAD_DIM = 128
MOBA_BLOCK = 256
MOBA_TOPK = 3
SSM_GROUP = 16
LN_EPS = 1e-5
NEG = -1e30
SB_NEGLIGIBLE_LOG = -120.0

MIB = 1024 * 1024


def _nt(a, b):
    return lax.dot_general(a, b, (((1,), (1,)), ((), ())), preferred_element_type=F32)


def _mm(a, b):
    return jnp.dot(a, b, preferred_element_type=F32)


def _dot3(x, u):
    hi = x.astype(BF16)
    r1 = x - hi.astype(F32)
    mid = r1.astype(BF16)
    lo = (r1 - mid.astype(F32)).astype(BF16)
    return _mm(hi, u) + _mm(mid, u) + _mm(lo, u)


def _params(sem, vmem_mib):
    return pltpu.CompilerParams(dimension_semantics=sem, vmem_limit_bytes=vmem_mib * MIB)


def _proj_kernel(x_ref, w_ref, o_ref, wb_ref):
    @pl.when(pl.program_id(1) == 0)
    def _():
        wb_ref[...] = w_ref[...].astype(BF16)

    o_ref[...] = _mm(x_ref[...], wb_ref[...])


def _in_proj(xb, w_all, layer, tm=512, tn=1024):
    m, k = xb.shape
    n = w_all.shape[2]
    assert m % tm == 0 and n % tn == 0
    return pl.pallas_call(
        _proj_kernel,
        grid=(n // tn, m // tm),
        in_specs=[pl.BlockSpec((tm, k), lambda j, i: (i, 0)),
                  pl.BlockSpec((None, k, tn), lambda j, i: (layer, 0, j))],
        out_specs=pl.BlockSpec((tm, tn), lambda j, i: (i, j)),
        out_shape=jax.ShapeDtypeStruct((m, n), F32),
        scratch_shapes=[pltpu.VMEM((k, tn), BF16)],
        compiler_params=_params(("arbitrary", "arbitrary"), 44),
        name="in_proj",
    )(xb, w_all)


def _sb_block(z, v_bf, ustrict, carry, acc, valid):
    lg = jnp.log(1.0 + jnp.exp(-jnp.abs(z)))
    log_keep = -(jnp.maximum(z, 0.0) + lg)
    log_beta = jnp.minimum(z, 0.0) - lg
    if valid is not None:
        log_keep = jnp.where(valid, log_keep, 0.0)
    later = _dot3(log_keep, ustrict) + carry
    w = jnp.exp(log_beta + later)
    if valid is not None:
        w = jnp.where(valid, w, 0.0)
    acc = acc + _mm(w.astype(BF16), v_bf)
    carry = carry + jnp.sum(log_keep, axis=1, keepdims=True)
    return carry, acc


def _strict_lower(n):
    r = lax.broadcasted_iota(jnp.int32, (n, n), 0)
    c = lax.broadcasted_iota(jnp.int32, (n, n), 1)
    return jnp.where(r > c, 1.0, 0.0).astype(BF16)


def _sb_prompt_kernel(q_ref, k_ref, v_ref, o_ref, *, tq, scale):
    qi = pl.program_id(2)
    q = q_ref[...].astype(BF16)
    ustrict = _strict_lower(tq)
    row = lax.broadcasted_iota(jnp.int32, (tq, tq), 0)
    col = lax.broadcasted_iota(jnp.int32, (tq, tq), 1)

    def block(kb, carry, acc, valid):
        start = pl.multiple_of(kb * tq, tq)
        k = k_ref[pl.ds(start, tq), :].astype(BF16)
        v = v_ref[pl.ds(start, tq), :].astype(BF16)
        z = _nt(q, k) * scale
        return _sb_block(z, v, ustrict, carry, acc, valid)

    carry = jnp.zeros((tq, 1), F32)
    acc = jnp.zeros((tq, HEAD_DIM), F32)
    carry, acc = block(qi, carry, acc, col < row)

    def cond(c):
        return jnp.logical_and(c[0] < qi, jnp.max(c[1]) > SB_NEGLIGIBLE_LOG)

    def body(c):
        carry, acc = block(qi - 1 - c[0], c[1], c[2], None)
        return c[0] + 1, carry, acc

    _, carry, acc = lax.while_loop(cond, body, (jnp.int32(0), carry, acc))
    o_ref[...] = acc


def _sb_prompt(p, batch, seq, heads, q_c0, k_c0, v_c0, tq=256):
    assert seq % tq == 0
    nq = seq // tq
    return pl.pallas_call(
        functools.partial(_sb_prompt_kernel, tq=tq, scale=HEAD_DIM ** -0.5),
        grid=(batch, heads, nq),
        in_specs=[pl.BlockSpec((tq, HEAD_DIM), lambda b, h, i: (b * nq + i, q_c0 + h)),
                  pl.BlockSpec((seq, HEAD_DIM), lambda b, h, i: (b, k_c0 + h)),
                  pl.BlockSpec((seq, HEAD_DIM), lambda b, h, i: (b, v_c0 + h))],
        out_specs=pl.BlockSpec((tq, HEAD_DIM), lambda b, h, i: (b * nq + i, h)),
        out_shape=jax.ShapeDtypeStruct((batch * seq, heads * HEAD_DIM), F32),
        compiler_params=_params(("arbitrary", "arbitrary", "arbitrary"), 32),
        name="sb_prompt",
    )(p, p, p)


def _page_copies(pt_ref, cache_ref, buf_ref, sem_ref, seq, slot, *, layer, n_pages, heads):
    copies = []
    for pg in range(n_pages):
        page_id = pt_ref[seq * n_pages + pg]
        for kv in range(2):
            for h in range(heads):
                copies.append(pltpu.make_async_copy(
                    cache_ref.at[layer, page_id, :, kv, h, :],
                    buf_ref.at[slot, pg, :, pl.ds((kv * heads + h) * HEAD_DIM, HEAD_DIM)],
                    sem_ref.at[slot]))
    return copies


def _fetch_pages(pt_ref, cache_ref, buf_ref, sem_ref, **kw):
    n = pl.program_id(0)
    slot = n % 2

    @pl.when(n == 0)
    def _():
        for c in _page_copies(pt_ref, cache_ref, buf_ref, sem_ref, 0, 0, **kw):
            c.start()

    @pl.when(n + 1 < pl.num_programs(0))
    def _():
        for c in _page_copies(pt_ref, cache_ref, buf_ref, sem_ref, n + 1, 1 - slot, **kw):
            c.start()

    for c in _page_copies(pt_ref, cache_ref, buf_ref, sem_ref, n, slot, **kw):
        c.wait()
    return slot


def _sb_sample_kernel(pt_ref, qbd_ref, kvn_ref, cache_ref, o_ref, buf_ref, sem_ref, *,
                      layer, n_pages, page, width, dec, scale):
    slot = _fetch_pages(pt_ref, cache_ref, buf_ref, sem_ref, layer=layer, n_pages=n_pages,
                        heads=width // HEAD_DIM)
    page_refs = [buf_ref.at[slot, pg] for pg in range(n_pages)]
    rows = qbd_ref.shape[0]
    qbd = qbd_ref[...].astype(BF16)
    ustrict = _strict_lower(page)
    i_row = lax.broadcasted_iota(jnp.int32, (rows, page), 0) % dec
    lane = lax.broadcasted_iota(jnp.int32, (rows, page), 1)

    kvn = kvn_ref[...]
    pad = jnp.zeros((page - kvn.shape[0], 2 * width), F32)
    kvn = jnp.concatenate([kvn, pad], axis=0).astype(BF16)
    carry = jnp.zeros((rows, 1), F32)
    acc = jnp.zeros((rows, width), F32)
    z = _nt(qbd, kvn[:, :width]) * scale
    carry, acc = _sb_block(z, kvn[:, width:], ustrict, carry, acc, lane < i_row)
    for pg in reversed(range(n_pages)):
        kv = page_refs[pg][...].astype(BF16)
        z = _nt(qbd, kv[:, :width]) * scale
        carry, acc = _sb_block(z, kv[:, width:], ustrict, carry, acc, None)
    o_ref[...] = acc


def _paged_call(kern, name, qbd, kv_new, extra, extra_specs, cache, page_table):
    n, rows, width = qbd.shape
    n_pages = page_table.shape[1]
    page = cache.shape[2]
    grid_spec = pltpu.PrefetchScalarGridSpec(
        num_scalar_prefetch=1,
        grid=(n,),
        in_specs=[pl.BlockSpec((None, rows, width), lambda i, pt: (i, 0, 0)),
                  pl.BlockSpec((None, kv_new.shape[1], 2 * width), lambda i, pt: (i, 0, 0))]
                 + extra_specs + [pl.BlockSpec(memory_space=pl.ANY)],
        out_specs=pl.BlockSpec((None, rows, width), lambda i, pt: (i, 0, 0)),
        scratch_shapes=[pltpu.VMEM((2, n_pages, page, 2 * width), F32), pltpu.SemaphoreType.DMA((2,))],
    )
    return pl.pallas_call(
        kern, grid_spec=grid_spec,
        out_shape=jax.ShapeDtypeStruct((n, rows, width), F32),
        compiler_params=_params(("arbitrary",), 48),
        name=name,
    )(page_table.reshape(-1), qbd, kv_new, *extra, cache)


def _sb_sample(qbd, kv_new, cache, page_table, layer, dec):
    kern = functools.partial(_sb_sample_kernel, layer=layer, n_pages=page_table.shape[1],
                             page=cache.shape[2], width=qbd.shape[2], dec=dec, scale=HEAD_DIM ** -0.5)
    return _paged_call(kern, "sb_sample", qbd, kv_new, [], [], cache, page_table)


def _bf_split(x):
    hi = x.astype(BF16)
    lo = (x - hi.astype(F32)).astype(BF16)
    return hi, lo


def _topk_bias_t(gate_t, own):
    nb = gate_t.shape[0]
    nidx = lax.broadcasted_iota(jnp.int32, gate_t.shape, 0)
    rank = jnp.zeros(gate_t.shape, F32)
    for mblk in range(nb):
        gm = gate_t[mblk:mblk + 1, :]
        beats = jnp.where(gm > gate_t, 1.0, jnp.where(gm == gate_t, jnp.where(nidx > mblk, 1.0, 0.0), 0.0))
        rank = rank + jnp.where(mblk < own, beats, 0.0)
    sel = jnp.where(nidx < own, jnp.where(rank < MOBA_TOPK, 1.0, 0.0), 0.0)
    return jnp.where(sel > 0.0, 0.0, NEG)


def _moba_prompt_kernel(slope_ref, q_ref, k_ref, v_ref, o_ref, kmean_ref, *, tq, nblk, scale):
    h = pl.program_id(1)
    qi = pl.program_id(2)
    slope = slope_ref[h]

    @pl.when(qi == 0)
    def _():
        kmean_ref[...] = jnp.zeros(kmean_ref.shape, F32)
        for nb in range(nblk):
            kb = k_ref[nb * tq:(nb + 1) * tq, :]
            kmean_ref[nb:nb + 1, :] = jnp.sum(kb, axis=0, keepdims=True) * (1.0 / tq)

    qf = q_ref[...]
    q = qf.astype(BF16)
    q_hi, q_lo = _bf_split(qf)
    m_hi, m_lo = _bf_split(kmean_ref[...])
    gate_t = _nt(m_hi, q_hi) + _nt(m_hi, q_lo) + _nt(m_lo, q_hi)
    bias_t = _topk_bias_t(gate_t, qi)
    bias_t = jnp.concatenate([bias_t, jnp.zeros((HEAD_DIM - bias_t.shape[0], tq), F32)], axis=0)
    r = lax.broadcasted_iota(jnp.int32, (tq, tq), 0)
    c = lax.broadcasted_iota(jnp.int32, (tq, tq), 1)
    eye = jnp.where(r == c, 1.0, 0.0).astype(BF16)
    bias_q = _nt(eye, bias_t.astype(BF16))
    qa = jnp.concatenate([q, bias_q.astype(BF16)], axis=1)
    rel = (r - c).astype(F32)

    start = pl.multiple_of(qi * tq, tq)
    k = k_ref[pl.ds(start, tq), :].astype(BF16)
    v = v_ref[pl.ds(start, tq), :].astype(BF16)
    s = _nt(q, k) * scale - slope * rel
    s = jnp.where(c <= r, s, NEG)
    m = jnp.max(s, axis=1, keepdims=True)
    p = jnp.exp(s - m)
    l = jnp.sum(p, axis=1, keepdims=True)
    acc = _mm(p.astype(BF16), v)
    lane = lax.broadcasted_iota(jnp.int32, (tq, HEAD_DIM), 1)

    def body(kb, carry):
        m, l, acc = carry
        start = pl.multiple_of(kb * tq, tq)
        k = k_ref[pl.ds(start, tq), :].astype(BF16)
        v = v_ref[pl.ds(start, tq), :].astype(BF16)
        onehot = jnp.where(lane == kb, 1.0, 0.0).astype(BF16)
        s = _nt(qa, jnp.concatenate([k, onehot], axis=1))
        dist = rel + ((qi - kb) * tq).astype(F32)
        s = s * scale - slope * dist
        m_new = jnp.maximum(m, jnp.max(s, axis=1, keepdims=True))
        alpha = jnp.exp(m - m_new)
        p = jnp.exp(s - m_new)
        l = alpha * l + jnp.sum(p, axis=1, keepdims=True)
        acc = alpha * acc + _mm(p.astype(BF16), v)
        return m_new, l, acc

    m, l, acc = lax.fori_loop(0, qi, body, (m, l, acc))
    o_ref[...] = acc / l


def _alibi_slopes(heads):
    return jnp.exp2(-8.0 * jnp.arange(1, heads + 1, dtype=F32) / heads)


def _moba_prompt(p, batch, seq, heads, q_c0, k_c0, v_c0):
    tq = MOBA_BLOCK
    assert seq % tq == 0 and seq // tq <= 8
    nq = seq // tq
    kern = functools.partial(_moba_prompt_kernel, tq=tq, nblk=nq, scale=HEAD_DIM ** -0.5)
    return pl.pallas_call(
        kern,
        grid=(batch, heads, nq),
        in_specs=[pl.BlockSpec(memory_space=pltpu.SMEM),
                  pl.BlockSpec((tq, HEAD_DIM), lambda b, h, i: (b * nq + i, q_c0 + h)),
                  pl.BlockSpec((seq, HEAD_DIM), lambda b, h, i: (b, k_c0 + h)),
                  pl.BlockSpec((seq, HEAD_DIM), lambda b, h, i: (b, v_c0 + h))],
        out_specs=pl.BlockSpec((tq, HEAD_DIM), lambda b, h, i: (b * nq + i, h)),
        out_shape=jax.ShapeDtypeStruct((batch * seq, heads * HEAD_DIM), F32),
        scratch_shapes=[pltpu.VMEM((8, HEAD_DIM), F32)],
        compiler_params=_params(("arbitrary", "arbitrary", "arbitrary"), 32),
        name="moba_prompt",
    )(_alibi_slopes(heads), p, p, p)


def _moba_sample_kernel(pt_ref, qbd_ref, kvn_ref, slope_ref, cache_ref, o_ref, buf_ref, sem_ref, *,
                        layer, n_pages, page, width, dec, scale):
    slot = _fetch_pages(pt_ref, cache_ref, buf_ref, sem_ref, layer=layer, n_pages=n_pages,
                        heads=width // HEAD_DIM)
    page_refs = [buf_ref.at[slot, pg] for pg in range(n_pages)]
    rows = qbd_ref.shape[0]
    per_blk = MOBA_BLOCK // page
    nblk = n_pages // per_blk
    past_len = n_pages * page
    qbd = qbd_ref[...].astype(BF16)
    slope = slope_ref[...]
    i_row = lax.broadcasted_iota(jnp.int32, (rows, page), 0) % dec
    lane = lax.broadcasted_iota(jnp.int32, (rows, page), 1)

    raw = [_nt(qbd, page_refs[pg][:, :width].astype(BF16)) for pg in range(n_pages)]
    gates = []
    for nb in range(nblk):
        tot = raw[nb * per_blk]
        for j in range(1, per_blk):
            tot = tot + raw[nb * per_blk + j]
        gates.append(jnp.sum(tot, axis=1, keepdims=True) * (1.0 / MOBA_BLOCK))
    sel_bias = []
    for nb in range(nblk):
        rank = jnp.zeros((rows, 1), F32)
        for mb in range(nblk):
            if mb == nb:
                continue
            beats = (gates[mb] >= gates[nb]) if mb < nb else (gates[mb] > gates[nb])
            rank = rank + jnp.where(beats, 1.0, 0.0)
        sel_bias.append(jnp.where(rank < MOBA_TOPK, 0.0, NEG))

    t_pos = (past_len + i_row).astype(F32)
    scores = []
    for pg in range(n_pages):
        dist = t_pos - (pg * page + lane).astype(F32)
        scores.append(raw[pg] * scale - slope * dist + sel_bias[pg // per_blk])
    kvn = kvn_ref[...]
    pad = jnp.zeros((page - kvn.shape[0], 2 * width), F32)
    kvn = jnp.concatenate([kvn, pad], axis=0).astype(BF16)
    s_new = _nt(qbd, kvn[:, :width]) * scale - slope * (i_row - lane).astype(F32)
    s_new = jnp.where(lane <= i_row, s_new, NEG)

    m = jnp.max(s_new, axis=1, keepdims=True)
    for s in scores:
        m = jnp.maximum(m, jnp.max(s, axis=1, keepdims=True))
    p_new = jnp.exp(s_new - m)
    l = jnp.sum(p_new, axis=1, keepdims=True)
    acc = _mm(p_new.astype(BF16), kvn[:, width:])
    for pg in range(n_pages):
        p = jnp.exp(scores[pg] - m)
        l = l + jnp.sum(p, axis=1, keepdims=True)
        acc = acc + _mm(p.astype(BF16), page_refs[pg][:, width:].astype(BF16))
    o_ref[...] = acc / l


def _moba_sample(qbd, kv_new, slope_rows, cache, page_table, layer, dec):
    rows = qbd.shape[1]
    n_pages = page_table.shape[1]
    page = cache.shape[2]
    assert MOBA_BLOCK % page == 0 and (n_pages * page) % MOBA_BLOCK == 0 and dec <= MOBA_BLOCK
    kern = functools.partial(_moba_sample_kernel, layer=layer, n_pages=n_pages, page=page,
                             width=qbd.shape[2], dec=dec, scale=HEAD_DIM ** -0.5)
    return _paged_call(kern, "moba_sample", qbd, kv_new, [slope_rows],
                       [pl.BlockSpec((rows, 1), lambda i, pt: (0, 0))], cache, page_table)


def _ssm_disc_kernel(are_ref, aim_ref, ls_ref, bre_ref, bim_ref, abr_ref, abi_ref, bbr_ref, bbi_ref):
    a_re = are_ref[...]
    a_im = aim_ref[...]
    step = jnp.exp(ls_ref[...])
    mag = jnp.exp(a_re * step)
    ang = a_im * step
    abar_re = mag * jnp.cos(ang)
    abar_im = mag * jnp.sin(ang)
    den = a_re * a_re + a_im * a_im
    num_re = abar_re - 1.0
    coef_re = (num_re * a_re + abar_im * a_im) / den
    coef_im = (abar_im * a_re - num_re * a_im) / den
    b_re = bre_ref[...]
    b_im = bim_ref[...]
    abr_ref[...] = abar_re
    abi_ref[...] = abar_im
    bbr_ref[...] = coef_re * b_re - coef_im * b_im
    bbi_ref[...] = coef_re * b_im + coef_im * b_re


def _ssm_discretise(a_re, a_im, log_step, b_re, b_im):
    g, pdim = a_re.shape
    hdim = b_re.shape[2]
    rep = lambda a: jnp.repeat(a, hdim, axis=0)
    ls = jnp.broadcast_to(log_step[:, None], (g, pdim))
    bt = lambda b: b.transpose(0, 2, 1).reshape(g * hdim, pdim)
    shp = jax.ShapeDtypeStruct((g * hdim, pdim), F32)
    abr, abi, bbr, bbi = pl.pallas_call(
        _ssm_disc_kernel, out_shape=(shp, shp, shp, shp), name="ssm_disc",
    )(rep(a_re), rep(a_im), rep(ls), bt(b_re), bt(b_im))
    abar = jnp.stack([abr[::hdim].reshape(-1), abi[::hdim].reshape(-1)])
    blockdiag = lambda m: (jnp.eye(g, dtype=F32)[:, None, :, None]
                           * m.reshape(g, hdim, 1, pdim)).reshape(g * hdim, g * pdim)
    bblk = jnp.concatenate([blockdiag(bbr), blockdiag(bbi)], axis=1)
    return abar, bblk


def _ssm_scan_kernel(u_ref, h0_ref, abar_ref, bblk_ref, cre_ref, cim_ref, d_ref, wglu_ref, bglu_ref,
                     o_ref, hlast_ref, h_ref, bu_ref, *, rows, steps, nstate):
    @pl.when(pl.program_id(0) == 0)
    def _():
        h_ref[...] = h0_ref[...]

    u = u_ref[...]
    bu_ref[...] = _mm(u.astype(BF16), bblk_ref[...])
    a_re = abar_ref[0:1, :]
    a_im = abar_ref[1:2, :]

    def advance(h_re, h_im, x_re, x_im):
        return a_re * h_re - a_im * h_im + x_re, a_re * h_im + a_im * h_re + x_im

    if rows % 8 == 0:
        def step(t, carry):
            r0 = pl.multiple_of(t * rows, rows)
            n_re, n_im = advance(h_ref[:, :nstate], h_ref[:, nstate:],
                                 bu_ref[pl.ds(r0, rows), :nstate], bu_ref[pl.ds(r0, rows), nstate:])
            h_ref[:, :nstate] = n_re
            h_ref[:, nstate:] = n_im
            bu_ref[pl.ds(r0, rows), :nstate] = n_re
            bu_ref[pl.ds(r0, rows), nstate:] = n_im
            return carry

        lax.fori_loop(0, steps, step, 0)
    else:
        assert rows == 4 and steps % 2 == 0
        top = lax.broadcasted_iota(jnp.int32, (8, nstate), 0) < 4

        def step2(t, carry):
            r0 = pl.multiple_of(t * 8, 8)
            x_re = bu_ref[pl.ds(r0, 8), :nstate]
            x_im = bu_ref[pl.ds(r0, 8), nstate:]
            e_re, e_im = advance(pltpu.roll(h_ref[:, :nstate], 4, axis=0),
                                 pltpu.roll(h_ref[:, nstate:], 4, axis=0), x_re, x_im)
            o_re, o_im = advance(pltpu.roll(e_re, 4, axis=0), pltpu.roll(e_im, 4, axis=0), x_re, x_im)
            n_re = jnp.where(top, e_re, o_re)
            n_im = jnp.where(top, e_im, o_im)
            h_ref[:, :nstate] = n_re
            h_ref[:, nstate:] = n_im
            bu_ref[pl.ds(r0, 8), :nstate] = n_re
            bu_ref[pl.ds(r0, 8), nstate:] = n_im
            return carry

        lax.fori_loop(0, steps // 2, step2, 0)
    hlast_ref[...] = h_ref[...]
    y = (_mm(bu_ref[:, :nstate].astype(BF16), cre_ref[...])
         - _mm(bu_ref[:, nstate:].astype(BF16), cim_ref[...]) + d_ref[...] * u)
    y = jax.nn.gelu(y)
    o_ref[...] = y * jax.nn.sigmoid(_mm(y.astype(BF16), wglu_ref[...]) + bglu_ref[...])


def _ssm_scan(u, h0, abar, bblk, cre, cim, d, wglu, bglu, rows, steps_per_chunk):
    total, wc = u.shape
    nstate = abar.shape[1]
    chunk = rows * steps_per_chunk
    assert total % chunk == 0
    srows = max(rows, 8)
    h0 = jnp.pad(h0, ((srows - rows, 0), (0, 0)))
    const = lambda shape: pl.BlockSpec(shape, lambda c: (0,) * len(shape))
    kern = functools.partial(_ssm_scan_kernel, rows=rows, steps=steps_per_chunk, nstate=nstate)
    y, h_last = pl.pallas_call(
        kern,
        grid=(total // chunk,),
        in_specs=[pl.BlockSpec((chunk, wc), lambda c: (c, 0)),
                  const((srows, 2 * nstate)), const((2, nstate)), const((wc, 2 * nstate)),
                  const((nstate, wc)), const((nstate, wc)), const((1, wc)), const((wc, wc)),
                  const((1, wc))],
        out_specs=[pl.BlockSpec((chunk, wc), lambda c: (c, 0)), const((srows, 2 * nstate))],
        out_shape=[jax.ShapeDtypeStruct((total, wc), F32),
                   jax.ShapeDtypeStruct((srows, 2 * nstate), F32)],
        scratch_shapes=[pltpu.VMEM((srows, 2 * nstate), F32), pltpu.VMEM((chunk, 2 * nstate), F32)],
        compiler_params=_params(("arbitrary",), 48),
        name="ssm_scan",
    )(u, h0, abar, bblk, cre, cim, d, wglu, bglu)
    return y, h_last[srows - rows:]


def _post_kernel(x_ref, osb_ref, omb_ref, oc_ref, gsb_ref, gmb_ref, gc_ref,
                 gl0, gl1, gl2, gl3, gl4, gl5, bg_ref, wb_ref, wo_ref, lng_ref, lnb_ref,
                 y_ref, yb_ref, *, w_sb, w_mb, alpha):
    dm = wo_ref.shape[0]
    half = dm // 2
    gls = ((gl0, gl1), (gl2, gl3), (gl4, gl5))
    acts = (osb_ref[...] * jax.nn.silu(gsb_ref[...]),
            omb_ref[...] * jax.nn.silu(gmb_ref[...]),
            oc_ref[...] * jax.nn.silu(gc_ref[...]))
    cuts = (0, w_sb, w_sb + w_mb, wb_ref.shape[0])
    merged = None
    for i in range(3):
        proj = _mm(acts[i].astype(BF16), wb_ref[cuts[i]:cuts[i + 1], :])
        logits = jnp.concatenate([gls[i][0][...], gls[i][1][...]], axis=1) + bg_ref[i:i + 1, :]
        term = jax.nn.sigmoid(logits) * proj
        merged = term if merged is None else merged + term
    del half
    yv = alpha * x_ref[...] + _mm(merged.astype(BF16), wo_ref[...])
    mu = jnp.mean(yv, axis=1, keepdims=True)
    cen = yv - mu
    var = jnp.mean(cen * cen, axis=1, keepdims=True)
    out = cen * lax.rsqrt(var + LN_EPS) * lng_ref[...] + lnb_ref[...]
    y_ref[...] = out
    yb_ref[...] = out.astype(BF16)


def _post(x, p, o_sb, o_mb, o_c, b_gate, wb, wo, ln_g, ln_b, cols, alpha, tm=256):
    m, dm = x.shape
    w_sb, w_mb, w_c = o_sb.shape[1], o_mb.shape[1], o_c.shape[1]
    half = dm // 2
    assert m % tm == 0 and cols["g_sb"] % w_sb == 0 and cols["g_mb"] % w_mb == 0
    assert cols["g_c"] % w_c == 0 and cols["gate"] % half == 0
    row = lambda width, blk: pl.BlockSpec((tm, width), lambda i: (i, blk))
    const = lambda shape: pl.BlockSpec(shape, lambda i: (0,) * len(shape), pipeline_mode=pl.Buffered(1))
    gl_specs = [row(half, cols["gate"] // half + j) for j in range(6)]
    kern = functools.partial(_post_kernel, w_sb=w_sb, w_mb=w_mb, alpha=alpha)
    return pl.pallas_call(
        kern,
        grid=(m // tm,),
        in_specs=[row(dm, 0), row(w_sb, 0), row(w_mb, 0), row(w_c, 0),
                  row(w_sb, cols["g_sb"] // w_sb), row(w_mb, cols["g_mb"] // w_mb),
                  row(w_c, cols["g_c"] // w_c)] + gl_specs
                 + [const((3, dm)), const(wb.shape), const(wo.shape), const((1, dm)), const((1, dm))],
        out_specs=[row(dm, 0), row(dm, 0)],
        out_shape=[jax.ShapeDtypeStruct((m, dm), F32), jax.ShapeDtypeStruct((m, dm), BF16)],
        compiler_params=_params(("arbitrary",), 56),
        name="post",
    )(x, o_sb, o_mb, o_c, p, p, p, *([p] * 6), b_gate, wb, wo, ln_g, ln_b)


def _kv_rows_kernel(*refs, depth, heads):
    o_ref = refs[2 * depth]
    layer = pl.program_id(0)
    for l in range(depth):
        @pl.when(layer == l)
        def _(l=l):
            for kv in range(2):
                src = refs[2 * l + kv]
                for h in range(heads):
                    o_ref[:, kv, h, :] = src[:, h * HEAD_DIM:(h + 1) * HEAD_DIM]


def _kv_rows(p_layers, k_col, heads, row0, n_rows, tm=256):
    depth = len(p_layers)
    width = heads * HEAD_DIM
    assert k_col % width == 0 and row0 % tm == 0 and n_rows % tm == 0
    nblk = n_rows // tm

    def spec(l, kv):
        def index(layer, i):
            idle = jnp.where(layer < l, 0, nblk - 1)
            return (row0 // tm + jnp.where(layer == l, i, idle), k_col // width + kv)
        return pl.BlockSpec((tm, width), index)

    return pl.pallas_call(
        functools.partial(_kv_rows_kernel, depth=depth, heads=heads),
        grid=(depth, nblk),
        in_specs=[spec(l, kv) for l in range(depth) for kv in range(2)],
        out_specs=pl.BlockSpec((None, tm, 2, heads, HEAD_DIM), lambda layer, i: (layer, i, 0, 0, 0)),
        out_shape=jax.ShapeDtypeStruct((depth, n_rows, 2, heads, HEAD_DIM), F32),
        compiler_params=_params(("arbitrary", "arbitrary"), 32),
        name="kv_rows",
    )(*[p for p in p_layers for _ in range(2)])


def _block_diag_queries(q, heads):
    n, dec, width = q.shape
    qh = q.reshape(n, dec, heads, HEAD_DIM)
    eye = jnp.eye(heads, dtype=q.dtype)
    out = jnp.einsum("nihd,hg->nhigd", qh, eye)
    return out.reshape(n, heads * dec, width)


def _diag_heads(o, heads, dec):
    n = o.shape[0]
    o5 = o.reshape(n, heads, dec, heads, HEAD_DIM)
    d = jnp.stack([o5[:, h, :, h, :] for h in range(heads)], axis=2)
    return d.reshape(n * dec, heads * HEAD_DIM)


def kernel(x_prompt, x_sample, cache_sb, cache_moba, state_ssm, page_table, w_in, b_gate, w_branch,
           w_out, ln_g, ln_b, ssm_a_re, ssm_a_im, ssm_log_step, ssm_b_re, ssm_b_im, ssm_c_re, ssm_c_im,
           ssm_d, ssm_w_glu, ssm_b_glu):
    batch, seq, dm = x_prompt.shape
    n_dec, dec, _ = x_sample.shape
    depth = w_in.shape[0]
    h_sb = cache_sb.shape[4]
    h_mb = cache_moba.shape[4]
    w_sb, w_mb = h_sb * HEAD_DIM, h_mb * HEAD_DIM
    groups, nst = ssm_a_re.shape[1], ssm_a_re.shape[2]
    w_c = groups * SSM_GROUP
    n_p, n_s = batch * seq, n_dec * dec
    alpha = (2 * depth) ** 0.25
    page = cache_sb.shape[2]

    c_sb, c_mb = 0, 4 * w_sb
    c_u = c_mb + 4 * w_mb
    cols = {"g_sb": c_sb + 3 * w_sb, "g_mb": c_mb + 3 * w_mb, "g_c": c_u + w_c, "gate": c_u + 2 * w_c}
    blk = lambda col: col // HEAD_DIM

    x = jnp.concatenate([x_prompt.reshape(n_p, dm), x_sample.reshape(n_s, dm)], axis=0)
    xb = x.astype(BF16)
    slope_rows = jnp.repeat(_alibi_slopes(h_mb), dec)[:, None]
    wb_bf = w_branch.astype(BF16)
    wo_bf = w_out.astype(BF16)
    wglu_bf = ssm_w_glu.astype(BF16)
    eye_g = jnp.eye(groups, dtype=F32)

    outs = {k: [] for k in ("ssm_p", "ssm_s")}
    p_layers = []
    for l in range(depth):
        p = _in_proj(xb, w_in, l)
        p_layers.append(p)
        kv_sb = p[n_p:, c_sb + w_sb:c_sb + 3 * w_sb]
        kv_mb = p[n_p:, c_mb + w_mb:c_mb + 3 * w_mb]

        o_sb_p = _sb_prompt(p, batch, seq, h_sb, blk(c_sb), blk(c_sb + w_sb), blk(c_sb + 2 * w_sb))
        o_mb_p = _moba_prompt(p, batch, seq, h_mb, blk(c_mb), blk(c_mb + w_mb), blk(c_mb + 2 * w_mb))

        pad_new = lambda kv: jnp.pad(kv.reshape(n_dec, dec, -1), ((0, 0), (0, 8 - dec), (0, 0)))
        q_sb_s = p[n_p:, c_sb:c_sb + w_sb].reshape(n_dec, dec, w_sb)
        q_mb_s = p[n_p:, c_mb:c_mb + w_mb].reshape(n_dec, dec, w_mb)
        o_sb_s = _sb_sample(_block_diag_queries(q_sb_s, h_sb), pad_new(kv_sb), cache_sb, page_table, l, dec)
        o_mb_s = _moba_sample(_block_diag_queries(q_mb_s, h_mb), pad_new(kv_mb), slope_rows, cache_moba,
                              page_table, l, dec)
        o_sb = jnp.concatenate([o_sb_p, _diag_heads(o_sb_s, h_sb, dec)], axis=0)
        o_mb = jnp.concatenate([o_mb_p, _diag_heads(o_mb_s, h_mb, dec)], axis=0)

        abar, bblk = _ssm_discretise(ssm_a_re[l], ssm_a_im[l], ssm_log_step[l], ssm_b_re[l], ssm_b_im[l])
        cblk = lambda cm: (eye_g[:, None, :, None] * cm.transpose(0, 2, 1)[:, :, None, :]
                           ).reshape(groups * nst, w_c).astype(BF16)
        ssm_w = (abar, bblk.astype(BF16), cblk(ssm_c_re[l]), cblk(ssm_c_im[l]), ssm_d[l].reshape(1, w_c),
                 wglu_bf[l], ssm_b_glu[l].reshape(1, w_c))
        u = p[:, c_u:c_u + w_c]
        u_p = u[:n_p].reshape(batch, seq, w_c).transpose(1, 0, 2).reshape(n_p, w_c)
        u_s = u[n_p:].reshape(n_dec, dec, w_c).transpose(1, 0, 2).reshape(n_s, w_c)
        y_p, hl_p = _ssm_scan(u_p, jnp.zeros((batch, 2 * groups * nst), F32), *ssm_w,
                              rows=batch, steps_per_chunk=128)
        st = state_ssm[l].reshape(n_dec, groups * nst, 2)
        h0_s = jnp.concatenate([st[..., 0], st[..., 1]], axis=1)
        y_s, hl_s = _ssm_scan(u_s, h0_s, *ssm_w, rows=n_dec, steps_per_chunk=dec)
        o_c = jnp.concatenate([y_p.reshape(seq, batch, w_c).transpose(1, 0, 2).reshape(n_p, w_c),
                               y_s.reshape(dec, n_dec, w_c).transpose(1, 0, 2).reshape(n_s, w_c)], axis=0)
        unstate = lambda hl: jnp.stack([hl[:, :groups * nst], hl[:, groups * nst:]], axis=-1
                                       ).reshape(hl.shape[0], groups, nst, 2)
        outs["ssm_p"].append(unstate(hl_p))
        outs["ssm_s"].append(unstate(hl_s))

        x, xb = _post(x, p, o_sb, o_mb, o_c, b_gate[l], wb_bf[l], wo_bf[l], ln_g[l].reshape(1, dm),
                      ln_b[l].reshape(1, dm), cols, alpha)

    kv_p = lambda col, heads: _kv_rows(p_layers, col, heads, 0, n_p).reshape(
        depth, batch, seq, 2, heads, HEAD_DIM)
    kv_s = lambda col, heads: _kv_rows(p_layers, col, heads, n_p, n_s).reshape(
        depth, n_dec, dec, 2, heads, HEAD_DIM)
    return (x[:n_p].reshape(batch, seq, dm), x[n_p:].reshape(n_dec, dec, dm),
            kv_p(c_sb + w_sb, h_sb), kv_p(c_mb + w_mb, h_mb), jnp.stack(outs["ssm_p"]),
            kv_s(c_sb + w_sb, h_sb), kv_s(c_mb + w_mb, h_mb), jnp.stack(outs["ssm_s"]))
```

```python
import functools

import jax
import jax.numpy as jnp
from jax import lax
from jax.experimental import pallas as pl
from jax.experimental.pallas import tpu as pltpu

F32 = jnp.float32
BF16 = jnp.bfloat16

HEAD_DIM = 128
MOBA_BLOCK = 256
MOBA_TOPK = 3
SSM_GROUP = 16
LN_EPS = 1e-5
NEG = -1e30
SB_NEGLIGIBLE_LOG = -120.0

MIB = 1024 * 1024


def _nt(a, b):
    return lax.dot_general(a, b, (((1,), (1,)), ((), ())), preferred_element_type=F32)


def _mm(a, b):
    return jnp.dot(a, b, preferred_element_type=F32)


def _dot3(x, u):
    hi = x.astype(BF16)
    r1 = x - hi.astype(F32)
    mid = r1.astype(BF16)
    lo = (r1 - mid.astype(F32)).astype(BF16)
    return _mm(hi, u) + _mm(mid, u) + _mm(lo, u)


def _params(sem, vmem_mib):
    return pltpu.CompilerParams(dimension_semantics=sem, vmem_limit_bytes=vmem_mib * MIB)


def _proj_kernel(x_ref, w_ref, o_ref, wb_ref):
    @pl.when(pl.program_id(1) == 0)
    def _():
        wb_ref[...] = w_ref[...].astype(BF16)

    o_ref[...] = _mm(x_ref[...], wb_ref[...])


def _in_proj(xb, w_all, layer, tm=512, tn=1024):
    m, k = xb.shape
    n = w_all.shape[2]
    assert m % tm == 0 and n % tn == 0
    return pl.pallas_call(
        _proj_kernel,
        grid=(n // tn, m // tm),
        in_specs=[pl.BlockSpec((tm, k), lambda j, i: (i, 0)),
                  pl.BlockSpec((None, k, tn), lambda j, i: (layer, 0, j))],
        out_specs=pl.BlockSpec((tm, tn), lambda j, i: (i, j)),
        out_shape=jax.ShapeDtypeStruct((m, n), F32),
        scratch_shapes=[pltpu.VMEM((k, tn), BF16)],
        compiler_params=_params(("arbitrary", "arbitrary"), 44),
        name="in_proj",
    )(xb, w_all)


def _sb_block(z, v_bf, ustrict, carry, acc, valid):
    lg = jnp.log(1.0 + jnp.exp(-jnp.abs(z)))
    log_keep = -(jnp.maximum(z, 0.0) + lg)
    log_beta = jnp.minimum(z, 0.0) - lg
    if valid is not None:
        log_keep = jnp.where(valid, log_keep, 0.0)
    later = _dot3(log_keep, ustrict) + carry
    w = jnp.exp(log_beta + later)
    if valid is not None:
        w = jnp.where(valid, w, 0.0)
    acc = acc + _mm(w.astype(BF16), v_bf)
    carry = carry + jnp.sum(log_keep, axis=1, keepdims=True)
    return carry, acc


def _strict_lower(n):
    r = lax.broadcasted_iota(jnp.int32, (n, n), 0)
    c = lax.broadcasted_iota(jnp.int32, (n, n), 1)
    return jnp.where(r > c, 1.0, 0.0).astype(BF16)


def _sb_prompt_kernel(q_ref, k_ref, v_ref, o_ref, *, tq, scale):
    qi = pl.program_id(2)
    q = q_ref[...].astype(BF16)
    ustrict = _strict_lower(tq)
    row = lax.broadcasted_iota(jnp.int32, (tq, tq), 0)
    col = lax.broadcasted_iota(jnp.int32, (tq, tq), 1)

    def block(kb, carry, acc, valid):
        start = pl.multiple_of(kb * tq, tq)
        k = k_ref[pl.ds(start, tq), :].astype(BF16)
        v = v_ref[pl.ds(start, tq), :].astype(BF16)
        z = _nt(q, k) * scale
        return _sb_block(z, v, ustrict, carry, acc, valid)

    carry = jnp.zeros((tq, 1), F32)
    acc = jnp.zeros((tq, HEAD_DIM), F32)
    carry, acc = block(qi, carry, acc, col < row)

    def cond(c):
        return jnp.logical_and(c[0] < qi, jnp.max(c[1]) > SB_NEGLIGIBLE_LOG)

    def body(c):
        carry, acc = block(qi - 1 - c[0], c[1], c[2], None)
        return c[0] + 1, carry, acc

    _, carry, acc = lax.while_loop(cond, body, (jnp.int32(0), carry, acc))
    o_ref[...] = acc


def _sb_prompt(p, batch, seq, heads, q_c0, k_c0, v_c0, tq=256):
    assert seq % tq == 0
    nq = seq // tq
    return pl.pallas_call(
        functools.partial(_sb_prompt_kernel, tq=tq, scale=HEAD_DIM ** -0.5),
        grid=(batch, heads, nq),
        in_specs=[pl.BlockSpec((tq, HEAD_DIM), lambda b, h, i: (b * nq + i, q_c0 + h)),
                  pl.BlockSpec((seq, HEAD_DIM), lambda b, h, i: (b, k_c0 + h)),
                  pl.BlockSpec((seq, HEAD_DIM), lambda b, h, i: (b, v_c0 + h))],
        out_specs=pl.BlockSpec((tq, HEAD_DIM), lambda b, h, i: (b * nq + i, h)),
        out_shape=jax.ShapeDtypeStruct((batch * seq, heads * HEAD_DIM), F32),
        compiler_params=_params(("arbitrary", "arbitrary", "arbitrary"), 32),
        name="sb_prompt",
    )(p, p, p)


def _page_copies(cache_ref, page_ids, dst_ref, sem, *, layer, heads):
    copies = []
    for j, page_id in enumerate(page_ids):
        for kv in range(2):
            for h in range(heads):
                copies.append(pltpu.make_async_copy(
                    cache_ref.at[layer, page_id, :, h, kv, :],
                    dst_ref.at[j, :, pl.ds((kv * heads + h) * HEAD_DIM, HEAD_DIM)], sem))
    return copies


def _fetch_pages(pt_ref, cache_ref, buf_ref, sem_ref, *, first, count, n_pages, layer, heads):
    n = pl.program_id(0)
    slot = n % 2

    def copies(seq, slot):
        ids = [pt_ref[seq * n_pages + first + j] for j in range(count)]
        return _page_copies(cache_ref, ids, buf_ref.at[slot], sem_ref.at[slot], layer=layer, heads=heads)

    @pl.when(n == 0)
    def _():
        for c in copies(0, 0):
            c.start()

    @pl.when(n + 1 < pl.num_programs(0))
    def _():
        for c in copies(n + 1, 1 - slot):
            c.start()

    for c in copies(n, slot):
        c.wait()
    return slot


def _sb_sample_kernel(pt_ref, qbd_ref, kvn_ref, cache_ref, o_ref, buf_ref, sem_ref, late_ref, late_sem, *,
                      layer, n_pages, group, page, width, dec, scale):
    heads = width // HEAD_DIM
    n_groups = n_pages // group
    slot = _fetch_pages(pt_ref, cache_ref, buf_ref, sem_ref, first=n_pages - group, count=group,
                        n_pages=n_pages, layer=layer, heads=heads)
    rows = qbd_ref.shape[0]
    qbd = qbd_ref[...].astype(BF16)
    ustrict = _strict_lower(page)
    i_row = lax.broadcasted_iota(jnp.int32, (rows, page), 0) % dec
    lane = lax.broadcasted_iota(jnp.int32, (rows, page), 1)

    def pages(ref, carry, acc):
        for j in reversed(range(group)):
            kv = ref[j].astype(BF16)
            z = _nt(qbd, kv[:, :width]) * scale
            carry, acc = _sb_block(z, kv[:, width:], ustrict, carry, acc, None)
        return carry, acc

    kvn = kvn_ref[...]
    pad = jnp.zeros((page - kvn.shape[0], 2 * width), F32)
    kvn = jnp.concatenate([kvn, pad], axis=0).astype(BF16)
    carry = jnp.zeros((rows, 1), F32)
    acc = jnp.zeros((rows, width), F32)
    z = _nt(qbd, kvn[:, :width]) * scale
    carry, acc = _sb_block(z, kvn[:, width:], ustrict, carry, acc, lane < i_row)
    carry, acc = pages(buf_ref.at[slot], carry, acc)

    seq = pl.program_id(0)

    def cond(c):
        return jnp.logical_and(c[0] >= 0, jnp.max(c[1]) > SB_NEGLIGIBLE_LOG)

    def body(c):
        g = c[0]
        ids = [pt_ref[seq * n_pages + g * group + j] for j in range(group)]
        copies = _page_copies(cache_ref, ids, late_ref, late_sem.at[0], layer=layer, heads=heads)
        for cp in copies:
            cp.start()
        for cp in copies:
            cp.wait()
        carry, acc = pages(late_ref, c[1], c[2])
        return g - 1, carry, acc

    _, carry, acc = lax.while_loop(cond, body, (jnp.int32(n_groups - 2), carry, acc))
    o_ref[...] = acc


def _paged_call(kern, name, qbd, kv_new, extra, extra_specs, cache, page_table, prefetched, extra_scratch):
    n, rows, width = qbd.shape
    page = cache.shape[2]
    grid_spec = pltpu.PrefetchScalarGridSpec(
        num_scalar_prefetch=1,
        grid=(n,),
        in_specs=[pl.BlockSpec((None, rows, width), lambda i, pt: (i, 0, 0)),
                  pl.BlockSpec((None, kv_new.shape[1], 2 * width), lambda i, pt: (i, 0, 0))]
                 + extra_specs + [pl.BlockSpec(memory_space=pl.ANY)],
        out_specs=pl.BlockSpec((None, rows, width), lambda i, pt: (i, 0, 0)),
        scratch_shapes=[pltpu.VMEM((2, prefetched, page, 2 * width), F32), pltpu.SemaphoreType.DMA((2,))]
                       + extra_scratch,
    )
    return pl.pallas_call(
        kern, grid_spec=grid_spec,
        out_shape=jax.ShapeDtypeStruct((n, rows, width), F32),
        compiler_params=_params(("arbitrary",), 48),
        name=name,
    )(page_table.reshape(-1), qbd, kv_new, *extra, cache)


def _sb_sample(qbd, kv_new, cache, page_table, layer, dec, group=2):
    n_pages, page, width = page_table.shape[1], cache.shape[2], qbd.shape[2]
    assert n_pages % group == 0
    kern = functools.partial(_sb_sample_kernel, layer=layer, n_pages=n_pages, group=group, page=page,
                             width=width, dec=dec, scale=HEAD_DIM ** -0.5)
    late = [pltpu.VMEM((group, page, 2 * width), F32), pltpu.SemaphoreType.DMA((1,))]
    return _paged_call(kern, "sb_sample", qbd, kv_new, [], [], cache, page_table, group, late)


def _bf_split(x):
    hi = x.astype(BF16)
    lo = (x - hi.astype(F32)).astype(BF16)
    return hi, lo


def _topk_bias_t(gate_t, own):
    nb = gate_t.shape[0]
    nidx = lax.broadcasted_iota(jnp.int32, gate_t.shape, 0)
    rank = jnp.zeros(gate_t.shape, F32)
    for mblk in range(nb):
        gm = gate_t[mblk:mblk + 1, :]
        beats = jnp.where(gm > gate_t, 1.0, jnp.where(gm == gate_t, jnp.where(nidx > mblk, 1.0, 0.0), 0.0))
        rank = rank + jnp.where(mblk < own, beats, 0.0)
    sel = jnp.where(nidx < own, jnp.where(rank < MOBA_TOPK, 1.0, 0.0), 0.0)
    return jnp.where(sel > 0.0, 0.0, NEG)


def _moba_prompt_kernel(slope_ref, q_ref, k_ref, v_ref, o_ref, kmean_ref, *, tq, nblk, scale):
    h = pl.program_id(1)
    qi = pl.program_id(2)
    slope = slope_ref[h]

    @pl.when(qi == 0)
    def _():
        kmean_ref[...] = jnp.zeros(kmean_ref.shape, F32)
        for nb in range(nblk):
            kb = k_ref[nb * tq:(nb + 1) * tq, :]
            kmean_ref[nb:nb + 1, :] = jnp.sum(kb, axis=0, keepdims=True) * (1.0 / tq)

    qf = q_ref[...]
    q = qf.astype(BF16)
    q_hi, q_lo = _bf_split(qf)
    m_hi, m_lo = _bf_split(kmean_ref[...])
    gate_t = _nt(m_hi, q_hi) + _nt(m_hi, q_lo) + _nt(m_lo, q_hi)
    bias_t = _topk_bias_t(gate_t, qi)
    bias_t = jnp.concatenate([bias_t, jnp.zeros((HEAD_DIM - bias_t.shape[0], tq), F32)], axis=0)
    r = lax.broadcasted_iota(jnp.int32, (tq, tq), 0)
    c = lax.broadcasted_iota(jnp.int32, (tq, tq), 1)
    eye = jnp.where(r == c, 1.0, 0.0).astype(BF16)
    bias_q = _nt(eye, bias_t.astype(BF16))
    qa = jnp.concatenate([q, bias_q.astype(BF16)], axis=1)
    rel = (r - c).astype(F32)

    start = pl.multiple_of(qi * tq, tq)
    k = k_ref[pl.ds(start, tq), :].astype(BF16)
    v = v_ref[pl.ds(start, tq), :].astype(BF16)
    s = _nt(q, k) * scale - slope * rel
    s = jnp.where(c <= r, s, NEG)
    m = jnp.max(s, axis=1, keepdims=True)
    p = jnp.exp(s - m)
    l = jnp.sum(p, axis=1, keepdims=True)
    acc = _mm(p.astype(BF16), v)
    lane = lax.broadcasted_iota(jnp.int32, (tq, HEAD_DIM), 1)

    def body(kb, carry):
        m, l, acc = carry
        start = pl.multiple_of(kb * tq, tq)
        k = k_ref[pl.ds(start, tq), :].astype(BF16)
        v = v_ref[pl.ds(start, tq), :].astype(BF16)
        onehot = jnp.where(lane == kb, 1.0, 0.0).astype(BF16)
        s = _nt(qa, jnp.concatenate([k, onehot], axis=1))
        dist = rel + ((qi - kb) * tq).astype(F32)
        s = s * scale - slope * dist
        m_new = jnp.maximum(m, jnp.max(s, axis=1, keepdims=True))
        alpha = jnp.exp(m - m_new)
        p = jnp.exp(s - m_new)
        l = alpha * l + jnp.sum(p, axis=1, keepdims=True)
        acc = alpha * acc + _mm(p.astype(BF16), v)
        return m_new, l, acc

    m, l, acc = lax.fori_loop(0, qi, body, (m, l, acc))
    o_ref[...] = acc / l


def _alibi_slopes(heads):
    return jnp.exp2(-8.0 * jnp.arange(1, heads + 1, dtype=F32) / heads)


def _moba_prompt(p, batch, seq, heads, q_c0, k_c0, v_c0):
    tq = MOBA_BLOCK
    assert seq % tq == 0 and seq // tq <= 8
    nq = seq // tq
    kern = functools.partial(_moba_prompt_kernel, tq=tq, nblk=nq, scale=HEAD_DIM ** -0.5)
    return pl.pallas_call(
        kern,
        grid=(batch, heads, nq),
        in_specs=[pl.BlockSpec(memory_space=pltpu.SMEM),
                  pl.BlockSpec((tq, HEAD_DIM), lambda b, h, i: (b * nq + i, q_c0 + h)),
                  pl.BlockSpec((seq, HEAD_DIM), lambda b, h, i: (b, k_c0 + h)),
                  pl.BlockSpec((seq, HEAD_DIM), lambda b, h, i: (b, v_c0 + h))],
        out_specs=pl.BlockSpec((tq, HEAD_DIM), lambda b, h, i: (b * nq + i, h)),
        out_shape=jax.ShapeDtypeStruct((batch * seq, heads * HEAD_DIM), F32),
        scratch_shapes=[pltpu.VMEM((8, HEAD_DIM), F32)],
        compiler_params=_params(("arbitrary", "arbitrary", "arbitrary"), 32),
        name="moba_prompt",
    )(_alibi_slopes(heads), p, p, p)


def _moba_sample_kernel(pt_ref, qbd_ref, kvn_ref, slope_ref, cache_ref, o_ref, buf_ref, sem_ref, *,
                        layer, n_pages, page, width, dec, scale):
    slot = _fetch_pages(pt_ref, cache_ref, buf_ref, sem_ref, first=0, count=n_pages, n_pages=n_pages,
                        layer=layer, heads=width // HEAD_DIM)
    page_refs = [buf_ref.at[slot, pg] for pg in range(n_pages)]
    rows = qbd_ref.shape[0]
    per_blk = MOBA_BLOCK // page
    nblk = n_pages // per_blk
    past_len = n_pages * page
    qbd = qbd_ref[...].astype(BF16)
    slope = slope_ref[...]
    i_row = lax.broadcasted_iota(jnp.int32, (rows, page), 0) % dec
    lane = lax.broadcasted_iota(jnp.int32, (rows, page), 1)

    raw = [_nt(qbd, page_refs[pg][:, :width].astype(BF16)) for pg in range(n_pages)]
    gates = []
    for nb in range(nblk):
        tot = raw[nb * per_blk]
        for j in range(1, per_blk):
            tot = tot + raw[nb * per_blk + j]
        gates.append(jnp.sum(tot, axis=1, keepdims=True) * (1.0 / MOBA_BLOCK))
    sel_bias = []
    for nb in range(nblk):
        rank = jnp.zeros((rows, 1), F32)
        for mb in range(nblk):
            if mb == nb:
                continue
            beats = (gates[mb] >= gates[nb]) if mb < nb else (gates[mb] > gates[nb])
            rank = rank + jnp.where(beats, 1.0, 0.0)
        sel_bias.append(jnp.where(rank < MOBA_TOPK, 0.0, NEG))

    t_pos = (past_len + i_row).astype(F32)
    scores = []
    for pg in range(n_pages):
        dist = t_pos - (pg * page + lane).astype(F32)
        scores.append(raw[pg] * scale - slope * dist + sel_bias[pg // per_blk])
    kvn = kvn_ref[...]
    pad = jnp.zeros((page - kvn.shape[0], 2 * width), F32)
    kvn = jnp.concatenate([kvn, pad], axis=0).astype(BF16)
    s_new = _nt(qbd, kvn[:, :width]) * scale - slope * (i_row - lane).astype(F32)
    s_new = jnp.where(lane <= i_row, s_new, NEG)

    m = jnp.max(s_new, axis=1, keepdims=True)
    for s in scores:
        m = jnp.maximum(m, jnp.max(s, axis=1, keepdims=True))
    p_new = jnp.exp(s_new - m)
    l = jnp.sum(p_new, axis=1, keepdims=True)
    acc = _mm(p_new.astype(BF16), kvn[:, width:])
    for pg in range(n_pages):
        p = jnp.exp(scores[pg] - m)
        l = l + jnp.sum(p, axis=1, keepdims=True)
        acc = acc + _mm(p.astype(BF16), page_refs[pg][:, width:].astype(BF16))
    o_ref[...] = acc / l


def _moba_sample(qbd, kv_new, slope_rows, cache, page_table, layer, dec):
    rows = qbd.shape[1]
    n_pages = page_table.shape[1]
    page = cache.shape[2]
    assert MOBA_BLOCK % page == 0 and (n_pages * page) % MOBA_BLOCK == 0 and dec <= MOBA_BLOCK
    kern = functools.partial(_moba_sample_kernel, layer=layer, n_pages=n_pages, page=page,
                             width=qbd.shape[2], dec=dec, scale=HEAD_DIM ** -0.5)
    return _paged_call(kern, "moba_sample", qbd, kv_new, [slope_rows],
                       [pl.BlockSpec((rows, 1), lambda i, pt: (0, 0))], cache, page_table, n_pages, [])


def _ssm_disc_kernel(are_ref, aim_ref, ls_ref, bre_ref, bim_ref, abr_ref, abi_ref, bbr_ref, bbi_ref):
    a_re = are_ref[...]
    a_im = aim_ref[...]
    step = jnp.exp(ls_ref[...])
    mag = jnp.exp(a_re * step)
    ang = a_im * step
    abar_re = mag * jnp.cos(ang)
    abar_im = mag * jnp.sin(ang)
    den = a_re * a_re + a_im * a_im
    num_re = abar_re - 1.0
    coef_re = (num_re * a_re + abar_im * a_im) / den
    coef_im = (abar_im * a_re - num_re * a_im) / den
    b_re = bre_ref[...]
    b_im = bim_ref[...]
    abr_ref[...] = abar_re
    abi_ref[...] = abar_im
    bbr_ref[...] = coef_re * b_re - coef_im * b_im
    bbi_ref[...] = coef_re * b_im + coef_im * b_re


def _ssm_discretise(a_re, a_im, log_step, b_re, b_im):
    g, pdim = a_re.shape
    hdim = b_re.shape[2]
    rep = lambda a: jnp.repeat(a, hdim, axis=0)
    ls = jnp.broadcast_to(log_step[:, None], (g, pdim))
    bt = lambda b: b.transpose(0, 2, 1).reshape(g * hdim, pdim)
    shp = jax.ShapeDtypeStruct((g * hdim, pdim), F32)
    abr, abi, bbr, bbi = pl.pallas_call(
        _ssm_disc_kernel, out_shape=(shp, shp, shp, shp), name="ssm_disc",
    )(rep(a_re), rep(a_im), rep(ls), bt(b_re), bt(b_im))
    abar = jnp.stack([abr[::hdim].reshape(-1), abi[::hdim].reshape(-1)])
    blockdiag = lambda m: (jnp.eye(g, dtype=F32)[:, None, :, None]
                           * m.reshape(g, hdim, 1, pdim)).reshape(g * hdim, g * pdim)
    bblk = jnp.concatenate([blockdiag(bbr), blockdiag(bbi)], axis=1)
    return abar, bblk


def _ssm_scan_kernel(u_ref, h0_ref, abar_ref, bblk_ref, cre_ref, cim_ref, d_ref, wglu_ref, bglu_ref,
                     o_ref, hlast_ref, h_ref, bu_ref, *, rows, steps, nstate):
    @pl.when(pl.program_id(0) == 0)
    def _():
        h_ref[...] = h0_ref[...]

    u = u_ref[...]
    bu_ref[...] = _mm(u.astype(BF16), bblk_ref[...])
    a_re = abar_ref[0:1, :]
    a_im = abar_ref[1:2, :]

    def advance(h_re, h_im, x_re, x_im):
        return a_re * h_re - a_im * h_im + x_re, a_re * h_im + a_im * h_re + x_im

    if rows % 8 == 0:
        def step(t, carry):
            r0 = pl.multiple_of(t * rows, rows)
            n_re, n_im = advance(h_ref[:, :nstate], h_ref[:, nstate:],
                                 bu_ref[pl.ds(r0, rows), :nstate], bu_ref[pl.ds(r0, rows), nstate:])
            h_ref[:, :nstate] = n_re
            h_ref[:, nstate:] = n_im
            bu_ref[pl.ds(r0, rows), :nstate] = n_re
            bu_ref[pl.ds(r0, rows), nstate:] = n_im
            return carry

        lax.fori_loop(0, steps, step, 0)
    else:
        assert rows == 4 and steps % 2 == 0
        top = lax.broadcasted_iota(jnp.int32, (8, nstate), 0) < 4

        def step2(t, carry):
            r0 = pl.multiple_of(t * 8, 8)
            x_re = bu_ref[pl.ds(r0, 8), :nstate]
            x_im = bu_ref[pl.ds(r0, 8), nstate:]
            e_re, e_im = advance(pltpu.roll(h_ref[:, :nstate], 4, axis=0),
                                 pltpu.roll(h_ref[:, nstate:], 4, axis=0), x_re, x_im)
            o_re, o_im = advance(pltpu.roll(e_re, 4, axis=0), pltpu.roll(e_im, 4, axis=0), x_re, x_im)
            n_re = jnp.where(top, e_re, o_re)
            n_im = jnp.where(top, e_im, o_im)
            h_ref[:, :nstate] = n_re
            h_ref[:, nstate:] = n_im
            bu_ref[pl.ds(r0, 8), :nstate] = n_re
            bu_ref[pl.ds(r0, 8), nstate:] = n_im
            return carry

        lax.fori_loop(0, steps // 2, step2, 0)
    hlast_ref[...] = h_ref[...]
    y = (_mm(bu_ref[:, :nstate].astype(BF16), cre_ref[...])
         - _mm(bu_ref[:, nstate:].astype(BF16), cim_ref[...]) + d_ref[...] * u)
    y = jax.nn.gelu(y)
    o_ref[...] = y * jax.nn.sigmoid(_mm(y.astype(BF16), wglu_ref[...]) + bglu_ref[...])


def _ssm_scan(u, h0, abar, bblk, cre, cim, d, wglu, bglu, rows, steps_per_chunk):
    total, wc = u.shape
    nstate = abar.shape[1]
    chunk = rows * steps_per_chunk
    assert total % chunk == 0
    srows = max(rows, 8)
    h0 = jnp.pad(h0, ((srows - rows, 0), (0, 0)))
    const = lambda shape: pl.BlockSpec(shape, lambda c: (0,) * len(shape))
    kern = functools.partial(_ssm_scan_kernel, rows=rows, steps=steps_per_chunk, nstate=nstate)
    y, h_last = pl.pallas_call(
        kern,
        grid=(total // chunk,),
        in_specs=[pl.BlockSpec((chunk, wc), lambda c: (c, 0)),
                  const((srows, 2 * nstate)), const((2, nstate)), const((wc, 2 * nstate)),
                  const((nstate, wc)), const((nstate, wc)), const((1, wc)), const((wc, wc)),
                  const((1, wc))],
        out_specs=[pl.BlockSpec((chunk, wc), lambda c: (c, 0)), const((srows, 2 * nstate))],
        out_shape=[jax.ShapeDtypeStruct((total, wc), F32),
                   jax.ShapeDtypeStruct((srows, 2 * nstate), F32)],
        scratch_shapes=[pltpu.VMEM((srows, 2 * nstate), F32), pltpu.VMEM((chunk, 2 * nstate), F32)],
        compiler_params=_params(("arbitrary",), 48),
        name="ssm_scan",
    )(u, h0, abar, bblk, cre, cim, d, wglu, bglu)
    return y, h_last[srows - rows:]


def _post_kernel(x_ref, osb_ref, omb_ref, oc_ref, gsb_ref, gmb_ref, gc_ref,
                 gl0, gl1, gl2, gl3, gl4, gl5, bg_ref, wb_ref, wo_ref, lng_ref, lnb_ref,
                 y_ref, yb_ref, *, w_sb, w_mb, alpha):
    gls = ((gl0, gl1), (gl2, gl3), (gl4, gl5))
    acts = (osb_ref[...] * jax.nn.silu(gsb_ref[...]),
            omb_ref[...] * jax.nn.silu(gmb_ref[...]),
            oc_ref[...] * jax.nn.silu(gc_ref[...]))
    cuts = (0, w_sb, w_sb + w_mb, wb_ref.shape[0])
    merged = None
    for i in range(3):
        proj = _mm(acts[i].astype(BF16), wb_ref[cuts[i]:cuts[i + 1], :])
        logits = jnp.concatenate([gls[i][0][...], gls[i][1][...]], axis=1) + bg_ref[i:i + 1, :]
        term = jax.nn.sigmoid(logits) * proj
        merged = term if merged is None else merged + term
    yv = alpha * x_ref[...] + _mm(merged.astype(BF16), wo_ref[...])
    mu = jnp.mean(yv, axis=1, keepdims=True)
    cen = yv - mu
    var = jnp.mean(cen * cen, axis=1, keepdims=True)
    out = cen * lax.rsqrt(var + LN_EPS) * lng_ref[...] + lnb_ref[...]
    y_ref[...] = out
    yb_ref[...] = out.astype(BF16)


def _post(x, p, o_sb, o_mb, o_c, b_gate, wb, wo, ln_g, ln_b, cols, alpha, tm=256):
    m, dm = x.shape
    w_sb, w_mb, w_c = o_sb.shape[1], o_mb.shape[1], o_c.shape[1]
    half = dm // 2
    assert m % tm == 0 and cols["g_sb"] % w_sb == 0 and cols["g_mb"] % w_mb == 0
    assert cols["g_c"] % w_c == 0 and cols["gate"] % half == 0
    row = lambda width, blk: pl.BlockSpec((tm, width), lambda i: (i, blk))
    const = lambda shape: pl.BlockSpec(shape, lambda i: (0,) * len(shape), pipeline_mode=pl.Buffered(1))
    gl_specs = [row(half, cols["gate"] // half + j) for j in range(6)]
    kern = functools.partial(_post_kernel, w_sb=w_sb, w_mb=w_mb, alpha=alpha)
    return pl.pallas_call(
        kern,
        grid=(m // tm,),
        in_specs=[row(dm, 0), row(w_sb, 0), row(w_mb, 0), row(w_c, 0),
                  row(w_sb, cols["g_sb"] // w_sb), row(w_mb, cols["g_mb"] // w_mb),
                  row(w_c, cols["g_c"] // w_c)] + gl_specs
                 + [const((3, dm)), const(wb.shape), const(wo.shape), const((1, dm)), const((1, dm))],
        out_specs=[row(dm, 0), row(dm, 0)],
        out_shape=[jax.ShapeDtypeStruct((m, dm), F32), jax.ShapeDtypeStruct((m, dm), BF16)],
        compiler_params=_params(("arbitrary",), 56),
        name="post",
    )(x, o_sb, o_mb, o_c, p, p, p, *([p] * 6), b_gate, wb, wo, ln_g, ln_b)


def _kv_rows_kernel(*refs, depth, heads):
    o_ref = refs[2 * depth]
    layer = pl.program_id(0)
    for l in range(depth):
        @pl.when(layer == l)
        def _(l=l):
            for kv in range(2):
                src = refs[2 * l + kv]
                for h in range(heads):
                    o_ref[:, h, kv, :] = src[:, h * HEAD_DIM:(h + 1) * HEAD_DIM]


def _kv_rows(p_layers, k_col, heads, row0, n_rows, tm=256):
    depth = len(p_layers)
    width = heads * HEAD_DIM
    assert k_col % width == 0 and row0 % tm == 0 and n_rows % tm == 0
    nblk = n_rows // tm

    def spec(l, kv):
        def index(layer, i):
            idle = jnp.where(layer < l, 0, nblk - 1)
            return (row0 // tm + jnp.where(layer == l, i, idle), k_col // width + kv)
        return pl.BlockSpec((tm, width), index)

    return pl.pallas_call(
        functools.partial(_kv_rows_kernel, depth=depth, heads=heads),
        grid=(depth, nblk),
        in_specs=[spec(l, kv) for l in range(depth) for kv in range(2)],
        out_specs=pl.BlockSpec((None, tm, heads, 2, HEAD_DIM), lambda layer, i: (layer, i, 0, 0, 0)),
        out_shape=jax.ShapeDtypeStruct((depth, n_rows, heads, 2, HEAD_DIM), F32),
        compiler_params=_params(("arbitrary", "arbitrary"), 32),
        name="kv_rows",
    )(*[p for p in p_layers for _ in range(2)])


def _block_diag_queries(q, heads):
    n, dec, width = q.shape
    qh = q.reshape(n, dec, heads, HEAD_DIM)
    eye = jnp.eye(heads, dtype=q.dtype)
    out = jnp.einsum("nihd,hg->nhigd", qh, eye)
    return out.reshape(n, heads * dec, width)


def _diag_heads(o, heads, dec):
    n = o.shape[0]
    o5 = o.reshape(n, heads, dec, heads, HEAD_DIM)
    d = jnp.stack([o5[:, h, :, h, :] for h in range(heads)], axis=2)
    return d.reshape(n * dec, heads * HEAD_DIM)


def kernel(x_prompt, x_sample, cache_sb, cache_moba, state_ssm, page_table, w_in, b_gate, w_branch,
           w_out, ln_g, ln_b, ssm_a_re, ssm_a_im, ssm_log_step, ssm_b_re, ssm_b_im, ssm_c_re, ssm_c_im,
           ssm_d, ssm_w_glu, ssm_b_glu):
    batch, seq, dm = x_prompt.shape
    n_dec, dec, _ = x_sample.shape
    depth = w_in.shape[0]
    h_sb = cache_sb.shape[4]
    h_mb = cache_moba.shape[4]
    w_sb, w_mb = h_sb * HEAD_DIM, h_mb * HEAD_DIM
    groups, nst = ssm_a_re.shape[1], ssm_a_re.shape[2]
    w_c = groups * SSM_GROUP
    n_p, n_s = batch * seq, n_dec * dec
    alpha = (2 * depth) ** 0.25

    c_sb, c_mb = 0, 4 * w_sb
    c_u = c_mb + 4 * w_mb
    cols = {"g_sb": c_sb + 3 * w_sb, "g_mb": c_mb + 3 * w_mb, "g_c": c_u + w_c, "gate": c_u + 2 * w_c}
    blk = lambda col: col // HEAD_DIM

    x = jnp.concatenate([x_prompt.reshape(n_p, dm), x_sample.reshape(n_s, dm)], axis=0)
    xb = x.astype(BF16)
    cache_sb_t = cache_sb.transpose(0, 1, 2, 4, 3, 5)
    cache_mb_t = cache_moba.transpose(0, 1, 2, 4, 3, 5)
    slope_rows = jnp.repeat(_alibi_slopes(h_mb), dec)[:, None]
    wb_bf = w_branch.astype(BF16)
    wo_bf = w_out.astype(BF16)
    wglu_bf = ssm_w_glu.astype(BF16)
    eye_g = jnp.eye(groups, dtype=F32)

    outs = {k: [] for k in ("ssm_p", "ssm_s")}
    p_layers = []
    for l in range(depth):
        p = _in_proj(xb, w_in, l)
        p_layers.append(p)
        kv_sb = p[n_p:, c_sb + w_sb:c_sb + 3 * w_sb]
        kv_mb = p[n_p:, c_mb + w_mb:c_mb + 3 * w_mb]

        o_sb_p = _sb_prompt(p, batch, seq, h_sb, blk(c_sb), blk(c_sb + w_sb), blk(c_sb + 2 * w_sb))
        o_mb_p = _moba_prompt(p, batch, seq, h_mb, blk(c_mb), blk(c_mb + w_mb), blk(c_mb + 2 * w_mb))

        pad_new = lambda kv: jnp.pad(kv.reshape(n_dec, dec, -1), ((0, 0), (0, 8 - dec), (0, 0)))
        q_sb_s = p[n_p:, c_sb:c_sb + w_sb].reshape(n_dec, dec, w_sb)
        q_mb_s = p[n_p:, c_mb:c_mb + w_mb].reshape(n_dec, dec, w_mb)
        o_sb_s = _sb_sample(_block_diag_queries(q_sb_s, h_sb), pad_new(kv_sb), cache_sb_t, page_table, l, dec)
        o_mb_s = _moba_sample(_block_diag_queries(q_mb_s, h_mb), pad_new(kv_mb), slope_rows, cache_mb_t,
                              page_table, l, dec)
        o_sb = jnp.concatenate([o_sb_p, _diag_heads(o_sb_s, h_sb, dec)], axis=0)
        o_mb = jnp.concatenate([o_mb_p, _diag_heads(o_mb_s, h_mb, dec)], axis=0)

        abar, bblk = _ssm_discretise(ssm_a_re[l], ssm_a_im[l], ssm_log_step[l], ssm_b_re[l], ssm_b_im[l])
        cblk = lambda cm: (eye_g[:, None, :, None] * cm.transpose(0, 2, 1)[:, :, None, :]
                           ).reshape(groups * nst, w_c).astype(BF16)
        ssm_w = (abar, bblk.astype(BF16), cblk(ssm_c_re[l]), cblk(ssm_c_im[l]), ssm_d[l].reshape(1, w_c),
                 wglu_bf[l], ssm_b_glu[l].reshape(1, w_c))
        u = p[:, c_u:c_u + w_c]
        u_p = u[:n_p].reshape(batch, seq, w_c).transpose(1, 0, 2).reshape(n_p, w_c)
        u_s = u[n_p:].reshape(n_dec, dec, w_c).transpose(1, 0, 2).reshape(n_s, w_c)
        y_p, hl_p = _ssm_scan(u_p, jnp.zeros((batch, 2 * groups * nst), F32), *ssm_w,
                              rows=batch, steps_per_chunk=128)
        st = state_ssm[l].reshape(n_dec, groups * nst, 2)
        h0_s = jnp.concatenate([st[..., 0], st[..., 1]], axis=1)
        y_s, hl_s = _ssm_scan(u_s, h0_s, *ssm_w, rows=n_dec, steps_per_chunk=dec)
        o_c = jnp.concatenate([y_p.reshape(seq, batch, w_c).transpose(1, 0, 2).reshape(n_p, w_c),
                               y_s.reshape(dec, n_dec, w_c).transpose(1, 0, 2).reshape(n_s, w_c)], axis=0)
        unstate = lambda hl: jnp.stack([hl[:, :groups * nst], hl[:, groups * nst:]], axis=-1
                                       ).reshape(hl.shape[0], groups, nst, 2)
        outs["ssm_p"].append(unstate(hl_p))
        outs["ssm_s"].append(unstate(hl_s))

        x, xb = _post(x, p, o_sb, o_mb, o_c, b_gate[l], wb_bf[l], wo_bf[l], ln_g[l].reshape(1, dm),
                      ln_b[l].reshape(1, dm), cols, alpha)

    kv_p = lambda col, heads: _kv_rows(p_layers, col, heads, 0, n_p).transpose(0, 1, 3, 2, 4).reshape(
        depth, batch, seq, 2, heads, HEAD_DIM)
    kv_s = lambda col, heads: _kv_rows(p_layers, col, heads, n_p, n_s).transpose(0, 1, 3, 2, 4).reshape(
        depth, n_dec, dec, 2, heads, HEAD_DIM)
    return (x[:n_p].reshape(batch, seq, dm), x[n_p:].reshape(n_dec, dec, dm),
            kv_p(c_sb + w_sb, h_sb), kv_p(c_mb + w_mb, h_mb), jnp.stack(outs["ssm_p"]),
            kv_s(c_sb + w_sb, h_sb), kv_s(c_mb + w_mb, h_mb), jnp.stack(outs["ssm_s"]))
```

```python
import functools

import jax
import jax.numpy as jnp
from jax import lax
from jax.experimental import pallas as pl
from jax.experimental.pallas import tpu as pltpu

F32 = jnp.float32
BF16 = jnp.bfloat16

HEAD_DIM = 128
MOBA_BLOCK = 256
MOBA_TOPK = 3
SSM_GROUP = 16
LN_EPS = 1e-5
NEG = -1e30
SB_NEGLIGIBLE_LOG = -120.0

MIB = 1024 * 1024


def _nt(a, b):
    return lax.dot_general(a, b, (((1,), (1,)), ((), ())), preferred_element_type=F32)


def _mm(a, b):
    return jnp.dot(a, b, preferred_element_type=F32)


def _dot3(x, u):
    hi = x.astype(BF16)
    r1 = x - hi.astype(F32)
    mid = r1.astype(BF16)
    lo = (r1 - mid.astype(F32)).astype(BF16)
    return _mm(hi, u) + _mm(mid, u) + _mm(lo, u)


def _params(sem, vmem_mib):
    return pltpu.CompilerParams(dimension_semantics=sem, vmem_limit_bytes=vmem_mib * MIB)


def _proj_kernel(x_ref, w_ref, o_ref, wb_ref):
    @pl.when(pl.program_id(1) == 0)
    def _():
        wb_ref[...] = w_ref[...].astype(BF16)

    o_ref[...] = _mm(x_ref[...], wb_ref[...])


def _in_proj(xb, w_all, layer, tm=512, tn=1024):
    m, k = xb.shape
    n = w_all.shape[2]
    assert m % tm == 0 and n % tn == 0
    return pl.pallas_call(
        _proj_kernel,
        grid=(n // tn, m // tm),
        in_specs=[pl.BlockSpec((tm, k), lambda j, i: (i, 0)),
                  pl.BlockSpec((None, k, tn), lambda j, i: (layer, 0, j))],
        out_specs=pl.BlockSpec((tm, tn), lambda j, i: (i, j)),
        out_shape=jax.ShapeDtypeStruct((m, n), F32),
        scratch_shapes=[pltpu.VMEM((k, tn), BF16)],
        compiler_params=_params(("arbitrary", "arbitrary"), 44),
        name="in_proj",
    )(xb, w_all)


def _sb_block(z, v_bf, ustrict, carry, acc, valid):
    lg = jnp.log(1.0 + jnp.exp(-jnp.abs(z)))
    log_keep = -(jnp.maximum(z, 0.0) + lg)
    log_beta = jnp.minimum(z, 0.0) - lg
    if valid is not None:
        log_keep = jnp.where(valid, log_keep, 0.0)
    later = _dot3(log_keep, ustrict) + carry
    w = jnp.exp(log_beta + later)
    if valid is not None:
        w = jnp.where(valid, w, 0.0)
    acc = acc + _mm(w.astype(BF16), v_bf)
    carry = carry + jnp.sum(log_keep, axis=1, keepdims=True)
    return carry, acc


def _strict_lower(n):
    r = lax.broadcasted_iota(jnp.int32, (n, n), 0)
    c = lax.broadcasted_iota(jnp.int32, (n, n), 1)
    return jnp.where(r > c, 1.0, 0.0).astype(BF16)


def _sb_prompt_kernel(q_ref, k_ref, v_ref, o_ref, *, tq, scale):
    hp = q_ref.shape[1] // HEAD_DIM
    qi = pl.program_id(2)
    cols = [slice(j * HEAD_DIM, (j + 1) * HEAD_DIM) for j in range(hp)]
    qs = [q_ref[:, cols[j]].astype(BF16) for j in range(hp)]
    ustrict = _strict_lower(tq)
    row = lax.broadcasted_iota(jnp.int32, (tq, tq), 0)
    col = lax.broadcasted_iota(jnp.int32, (tq, tq), 1)

    def block(kb, state, valid):
        start = pl.multiple_of(kb * tq, tq)
        out = []
        for j in range(hp):
            k = k_ref[pl.ds(start, tq), cols[j]].astype(BF16)
            v = v_ref[pl.ds(start, tq), cols[j]].astype(BF16)
            z = _nt(qs[j], k) * scale
            out.append(_sb_block(z, v, ustrict, state[j][0], state[j][1], valid))
        return tuple(out)

    zero = (jnp.zeros((tq, 1), F32), jnp.zeros((tq, HEAD_DIM), F32))
    state = block(qi, (zero,) * hp, col < row)

    def cond(c):
        top = jnp.max(c[1][0][0])
        for j in range(1, hp):
            top = jnp.maximum(top, jnp.max(c[1][j][0]))
        return jnp.logical_and(c[0] < qi, top > SB_NEGLIGIBLE_LOG)

    def body(c):
        return c[0] + 1, block(qi - 1 - c[0], c[1], None)

    _, state = lax.while_loop(cond, body, (jnp.int32(0), state))
    for j in range(hp):
        o_ref[:, cols[j]] = state[j][1]


def _sb_prompt(p, batch, seq, heads, q_c0, k_c0, v_c0, tq=256):
    assert seq % tq == 0
    nq = seq // tq
    hp = 2 if (heads % 2 == 0 and q_c0 % 2 == 0 and k_c0 % 2 == 0 and v_c0 % 2 == 0) else 1
    wide = hp * HEAD_DIM
    return pl.pallas_call(
        functools.partial(_sb_prompt_kernel, tq=tq, scale=HEAD_DIM ** -0.5),
        grid=(batch, heads // hp, nq),
        in_specs=[pl.BlockSpec((tq, wide), lambda b, h, i: (b * nq + i, q_c0 // hp + h)),
                  pl.BlockSpec((seq, wide), lambda b, h, i: (b, k_c0 // hp + h)),
                  pl.BlockSpec((seq, wide), lambda b, h, i: (b, v_c0 // hp + h))],
        out_specs=pl.BlockSpec((tq, wide), lambda b, h, i: (b * nq + i, h)),
        out_shape=jax.ShapeDtypeStruct((batch * seq, heads * HEAD_DIM), F32),
        compiler_params=_params(("arbitrary", "arbitrary", "arbitrary"), 32),
        name="sb_prompt",
    )(p, p, p)


def _page_copies(cache_ref, page_ids, dst_ref, sem, *, layer, heads):
    copies = []
    for j, page_id in enumerate(page_ids):
        for kv in range(2):
            for h in range(heads):
                copies.append(pltpu.make_async_copy(
                    cache_ref.at[layer, page_id, :, h, kv, :],
                    dst_ref.at[j, :, pl.ds((kv * heads + h) * HEAD_DIM, HEAD_DIM)], sem))
    return copies


def _fetch_pages(pt_ref, cache_ref, buf_ref, sem_ref, *, first, count, n_pages, layer, heads):
    n = pl.program_id(0)
    slot = n % 2

    def copies(seq, slot):
        ids = [pt_ref[seq * n_pages + first + j] for j in range(count)]
        return _page_copies(cache_ref, ids, buf_ref.at[slot], sem_ref.at[slot], layer=layer, heads=heads)

    @pl.when(n == 0)
    def _():
        for c in copies(0, 0):
            c.start()

    @pl.when(n + 1 < pl.num_programs(0))
    def _():
        for c in copies(n + 1, 1 - slot):
            c.start()

    for c in copies(n, slot):
        c.wait()
    return slot


def _sb_sample_kernel(pt_ref, qbd_ref, kvn_ref, cache_ref, o_ref, buf_ref, sem_ref, late_ref, late_sem, *,
                      layer, n_pages, group, page, width, dec, scale):
    heads = width // HEAD_DIM
    n_groups = n_pages // group
    slot = _fetch_pages(pt_ref, cache_ref, buf_ref, sem_ref, first=n_pages - group, count=group,
                        n_pages=n_pages, layer=layer, heads=heads)
    rows = qbd_ref.shape[0]
    qbd = qbd_ref[...].astype(BF16)
    ustrict = _strict_lower(page)
    i_row = lax.broadcasted_iota(jnp.int32, (rows, page), 0) % dec
    lane = lax.broadcasted_iota(jnp.int32, (rows, page), 1)

    def pages(ref, carry, acc):
        for j in reversed(range(group)):
            kv = ref[j].astype(BF16)
            z = _nt(qbd, kv[:, :width]) * scale
            carry, acc = _sb_block(z, kv[:, width:], ustrict, carry, acc, None)
        return carry, acc

    kvn = kvn_ref[...]
    pad = jnp.zeros((page - kvn.shape[0], 2 * width), F32)
    kvn = jnp.concatenate([kvn, pad], axis=0).astype(BF16)
    carry = jnp.zeros((rows, 1), F32)
    acc = jnp.zeros((rows, width), F32)
    z = _nt(qbd, kvn[:, :width]) * scale
    carry, acc = _sb_block(z, kvn[:, width:], ustrict, carry, acc, lane < i_row)
    carry, acc = pages(buf_ref.at[slot], carry, acc)

    seq = pl.program_id(0)

    def cond(c):
        return jnp.logical_and(c[0] >= 0, jnp.max(c[1]) > SB_NEGLIGIBLE_LOG)

    def body(c):
        g = c[0]
        ids = [pt_ref[seq * n_pages + g * group + j] for j in range(group)]
        copies = _page_copies(cache_ref, ids, late_ref, late_sem.at[0], layer=layer, heads=heads)
        for cp in copies:
            cp.start()
        for cp in copies:
            cp.wait()
        carry, acc = pages(late_ref, c[1], c[2])
        return g - 1, carry, acc

    _, carry, acc = lax.while_loop(cond, body, (jnp.int32(n_groups - 2), carry, acc))
    o_ref[...] = acc


def _paged_call(kern, name, qbd, kv_new, extra, extra_specs, cache, page_table, prefetched, extra_scratch):
    n, rows, width = qbd.shape
    page = cache.shape[2]
    grid_spec = pltpu.PrefetchScalarGridSpec(
        num_scalar_prefetch=1,
        grid=(n,),
        in_specs=[pl.BlockSpec((None, rows, width), lambda i, pt: (i, 0, 0)),
                  pl.BlockSpec((None, kv_new.shape[1], 2 * width), lambda i, pt: (i, 0, 0))]
                 + extra_specs + [pl.BlockSpec(memory_space=pl.ANY)],
        out_specs=pl.BlockSpec((None, rows, width), lambda i, pt: (i, 0, 0)),
        scratch_shapes=[pltpu.VMEM((2, prefetched, page, 2 * width), F32), pltpu.SemaphoreType.DMA((2,))]
                       + extra_scratch,
    )
    return pl.pallas_call(
        kern, grid_spec=grid_spec,
        out_shape=jax.ShapeDtypeStruct((n, rows, width), F32),
        compiler_params=_params(("arbitrary",), 48),
        name=name,
    )(page_table.reshape(-1), qbd, kv_new, *extra, cache)


def _sb_sample(qbd, kv_new, cache, page_table, layer, dec, group=2):
    n_pages, page, width = page_table.shape[1], cache.shape[2], qbd.shape[2]
    assert n_pages % group == 0
    kern = functools.partial(_sb_sample_kernel, layer=layer, n_pages=n_pages, group=group, page=page,
                             width=width, dec=dec, scale=HEAD_DIM ** -0.5)
    late = [pltpu.VMEM((group, page, 2 * width), F32), pltpu.SemaphoreType.DMA((1,))]
    return _paged_call(kern, "sb_sample", qbd, kv_new, [], [], cache, page_table, group, late)


def _bf_split(x):
    hi = x.astype(BF16)
    lo = (x - hi.astype(F32)).astype(BF16)
    return hi, lo


def _topk_bias_t(gate_t, own):
    nb = gate_t.shape[0]
    nidx = lax.broadcasted_iota(jnp.int32, gate_t.shape, 0)
    rank = jnp.zeros(gate_t.shape, F32)
    for mblk in range(nb):
        gm = gate_t[mblk:mblk + 1, :]
        beats = jnp.where(gm > gate_t, 1.0, jnp.where(gm == gate_t, jnp.where(nidx > mblk, 1.0, 0.0), 0.0))
        rank = rank + jnp.where(mblk < own, beats, 0.0)
    sel = jnp.where(nidx < own, jnp.where(rank < MOBA_TOPK, 1.0, 0.0), 0.0)
    return jnp.where(sel > 0.0, 0.0, NEG)


def _moba_prompt_kernel(slope_ref, q_ref, k_ref, v_ref, o_ref, kmean_ref, *, tq, nblk, scale):
    hp = q_ref.shape[1] // HEAD_DIM
    hg = pl.program_id(1)
    qi = pl.program_id(2)
    cols = [slice(j * HEAD_DIM, (j + 1) * HEAD_DIM) for j in range(hp)]
    slopes = [slope_ref[hg * hp + j] for j in range(hp)]

    @pl.when(qi == 0)
    def _():
        kmean_ref[...] = jnp.zeros(kmean_ref.shape, F32)
        for j in range(hp):
            for nb in range(nblk):
                kb = k_ref[nb * tq:(nb + 1) * tq, cols[j]]
                kmean_ref[8 * j + nb:8 * j + nb + 1, :] = jnp.sum(kb, axis=0, keepdims=True) * (1.0 / tq)

    r = lax.broadcasted_iota(jnp.int32, (tq, tq), 0)
    c = lax.broadcasted_iota(jnp.int32, (tq, tq), 1)
    eye = jnp.where(r == c, 1.0, 0.0).astype(BF16)
    rel = (r - c).astype(F32)
    lane = lax.broadcasted_iota(jnp.int32, (tq, HEAD_DIM), 1)
    start = pl.multiple_of(qi * tq, tq)

    qas, carries = [], []
    for j in range(hp):
        qf = q_ref[:, cols[j]]
        q = qf.astype(BF16)
        q_hi, q_lo = _bf_split(qf)
        m_hi, m_lo = _bf_split(kmean_ref[8 * j:8 * j + 8, :])
        gate_t = _nt(m_hi, q_hi) + _nt(m_hi, q_lo) + _nt(m_lo, q_hi)
        bias_t = _topk_bias_t(gate_t, qi)
        bias_t = jnp.concatenate([bias_t, jnp.zeros((HEAD_DIM - bias_t.shape[0], tq), F32)], axis=0)
        bias_q = _nt(eye, bias_t.astype(BF16))
        qas.append(jnp.concatenate([q, bias_q.astype(BF16)], axis=1))

        k = k_ref[pl.ds(start, tq), cols[j]].astype(BF16)
        v = v_ref[pl.ds(start, tq), cols[j]].astype(BF16)
        s = _nt(q, k) * scale - slopes[j] * rel
        s = jnp.where(c <= r, s, NEG)
        m = jnp.max(s, axis=1, keepdims=True)
        p = jnp.exp(s - m)
        carries.append((m, jnp.sum(p, axis=1, keepdims=True), _mm(p.astype(BF16), v)))

    def body(kb, carry):
        kstart = pl.multiple_of(kb * tq, tq)
        onehot = jnp.where(lane == kb, 1.0, 0.0).astype(BF16)
        dist = rel + ((qi - kb) * tq).astype(F32)
        out = []
        for j in range(hp):
            m, l, acc = carry[j]
            k = k_ref[pl.ds(kstart, tq), cols[j]].astype(BF16)
            v = v_ref[pl.ds(kstart, tq), cols[j]].astype(BF16)
            s = _nt(qas[j], jnp.concatenate([k, onehot], axis=1))
            s = s * scale - slopes[j] * dist
            m_new = jnp.maximum(m, jnp.max(s, axis=1, keepdims=True))
            alpha = jnp.exp(m - m_new)
            p = jnp.exp(s - m_new)
            l = alpha * l + jnp.sum(p, axis=1, keepdims=True)
            acc = alpha * acc + _mm(p.astype(BF16), v)
            out.append((m_new, l, acc))
        return tuple(out)

    carries = lax.fori_loop(0, qi, body, tuple(carries))
    for j in range(hp):
        o_ref[:, cols[j]] = carries[j][2] / carries[j][1]


def _alibi_slopes(heads):
    return jnp.exp2(-8.0 * jnp.arange(1, heads + 1, dtype=F32) / heads)


def _moba_prompt(p, batch, seq, heads, q_c0, k_c0, v_c0):
    tq = MOBA_BLOCK
    assert seq % tq == 0 and seq // tq <= 8
    nq = seq // tq
    hp = 2 if (heads % 2 == 0 and q_c0 % 2 == 0 and k_c0 % 2 == 0 and v_c0 % 2 == 0) else 1
    wide = hp * HEAD_DIM
    kern = functools.partial(_moba_prompt_kernel, tq=tq, nblk=nq, scale=HEAD_DIM ** -0.5)
    return pl.pallas_call(
        kern,
        grid=(batch, heads // hp, nq),
        in_specs=[pl.BlockSpec(memory_space=pltpu.SMEM),
                  pl.BlockSpec((tq, wide), lambda b, h, i: (b * nq + i, q_c0 // hp + h)),
                  pl.BlockSpec((seq, wide), lambda b, h, i: (b, k_c0 // hp + h)),
                  pl.BlockSpec((seq, wide), lambda b, h, i: (b, v_c0 // hp + h))],
        out_specs=pl.BlockSpec((tq, wide), lambda b, h, i: (b * nq + i, h)),
        out_shape=jax.ShapeDtypeStruct((batch * seq, heads * HEAD_DIM), F32),
        scratch_shapes=[pltpu.VMEM((8 * hp, HEAD_DIM), F32)],
        compiler_params=_params(("arbitrary", "arbitrary", "arbitrary"), 32),
        name="moba_prompt",
    )(_alibi_slopes(heads), p, p, p)


def _moba_sample_kernel(pt_ref, qbd_ref, kvn_ref, slope_ref, cache_ref, o_ref, buf_ref, sem_ref, *,
                        layer, n_pages, page, width, dec, scale):
    slot = _fetch_pages(pt_ref, cache_ref, buf_ref, sem_ref, first=0, count=n_pages, n_pages=n_pages,
                        layer=layer, heads=width // HEAD_DIM)
    page_refs = [buf_ref.at[slot, pg] for pg in range(n_pages)]
    rows = qbd_ref.shape[0]
    per_blk = MOBA_BLOCK // page
    nblk = n_pages // per_blk
    past_len = n_pages * page
    qbd = qbd_ref[...].astype(BF16)
    slope = slope_ref[...]
    i_row = lax.broadcasted_iota(jnp.int32, (rows, page), 0) % dec
    lane = lax.broadcasted_iota(jnp.int32, (rows, page), 1)

    raw = [_nt(qbd, page_refs[pg][:, :width].astype(BF16)) for pg in range(n_pages)]
    gates = []
    for nb in range(nblk):
        tot = raw[nb * per_blk]
        for j in range(1, per_blk):
            tot = tot + raw[nb * per_blk + j]
        gates.append(jnp.sum(tot, axis=1, keepdims=True) * (1.0 / MOBA_BLOCK))
    sel_bias = []
    for nb in range(nblk):
        rank = jnp.zeros((rows, 1), F32)
        for mb in range(nblk):
            if mb == nb:
                continue
            beats = (gates[mb] >= gates[nb]) if mb < nb else (gates[mb] > gates[nb])
            rank = rank + jnp.where(beats, 1.0, 0.0)
        sel_bias.append(jnp.where(rank < MOBA_TOPK, 0.0, NEG))

    t_pos = (past_len + i_row).astype(F32)
    scores = []
    for pg in range(n_pages):
        dist = t_pos - (pg * page + lane).astype(F32)
        scores.append(raw[pg] * scale - slope * dist + sel_bias[pg // per_blk])
    kvn = kvn_ref[...]
    pad = jnp.zeros((page - kvn.shape[0], 2 * width), F32)
    kvn = jnp.concatenate([kvn, pad], axis=0).astype(BF16)
    s_new = _nt(qbd, kvn[:, :width]) * scale - slope * (i_row - lane).astype(F32)
    s_new = jnp.where(lane <= i_row, s_new, NEG)

    m = jnp.max(s_new, axis=1, keepdims=True)
    for s in scores:
        m = jnp.maximum(m, jnp.max(s, axis=1, keepdims=True))
    p_new = jnp.exp(s_new - m)
    l = jnp.sum(p_new, axis=1, keepdims=True)
    acc = _mm(p_new.astype(BF16), kvn[:, width:])
    for pg in range(n_pages):
        p = jnp.exp(scores[pg] - m)
        l = l + jnp.sum(p, axis=1, keepdims=True)
        acc = acc + _mm(p.astype(BF16), page_refs[pg][:, width:].astype(BF16))
    o_ref[...] = acc / l


def _moba_sample(qbd, kv_new, slope_rows, cache, page_table, layer, dec):
    rows = qbd.shape[1]
    n_pages = page_table.shape[1]
    page = cache.shape[2]
    assert MOBA_BLOCK % page == 0 and (n_pages * page) % MOBA_BLOCK == 0 and dec <= MOBA_BLOCK
    kern = functools.partial(_moba_sample_kernel, layer=layer, n_pages=n_pages, page=page,
                             width=qbd.shape[2], dec=dec, scale=HEAD_DIM ** -0.5)
    return _paged_call(kern, "moba_sample", qbd, kv_new, [slope_rows],
                       [pl.BlockSpec((rows, 1), lambda i, pt: (0, 0))], cache, page_table, n_pages, [])


def _ssm_disc_kernel(are_ref, aim_ref, ls_ref, bre_ref, bim_ref, abr_ref, abi_ref, bbr_ref, bbi_ref):
    a_re = are_ref[...]
    a_im = aim_ref[...]
    step = jnp.exp(ls_ref[...])
    mag = jnp.exp(a_re * step)
    ang = a_im * step
    abar_re = mag * jnp.cos(ang)
    abar_im = mag * jnp.sin(ang)
    den = a_re * a_re + a_im * a_im
    num_re = abar_re - 1.0
    coef_re = (num_re * a_re + abar_im * a_im) / den
    coef_im = (abar_im * a_re - num_re * a_im) / den
    b_re = bre_ref[...]
    b_im = bim_ref[...]
    abr_ref[...] = abar_re
    abi_ref[...] = abar_im
    bbr_ref[...] = coef_re * b_re - coef_im * b_im
    bbi_ref[...] = coef_re * b_im + coef_im * b_re


def _ssm_discretise(a_re, a_im, log_step, b_re, b_im):
    g, pdim = a_re.shape
    hdim = b_re.shape[2]
    rep = lambda a: jnp.repeat(a, hdim, axis=0)
    ls = jnp.broadcast_to(log_step[:, None], (g, pdim))
    bt = lambda b: b.transpose(0, 2, 1).reshape(g * hdim, pdim)
    shp = jax.ShapeDtypeStruct((g * hdim, pdim), F32)
    abr, abi, bbr, bbi = pl.pallas_call(
        _ssm_disc_kernel, out_shape=(shp, shp, shp, shp), name="ssm_disc",
    )(rep(a_re), rep(a_im), rep(ls), bt(b_re), bt(b_im))
    abar = jnp.stack([abr[::hdim].reshape(-1), abi[::hdim].reshape(-1)])
    blockdiag = lambda m: (jnp.eye(g, dtype=F32)[:, None, :, None]
                           * m.reshape(g, hdim, 1, pdim)).reshape(g * hdim, g * pdim)
    bblk = jnp.concatenate([blockdiag(bbr), blockdiag(bbi)], axis=1)
    return abar, bblk


def _ssm_scan_kernel(u_ref, h0_ref, abar_ref, bblk_ref, cre_ref, cim_ref, d_ref, wglu_ref, bglu_ref,
                     o_ref, hlast_ref, h_ref, bu_ref, *, rows, steps, nstate):
    @pl.when(pl.program_id(0) == 0)
    def _():
        h_ref[...] = h0_ref[...]

    u = u_ref[...]
    bu_ref[...] = _mm(u.astype(BF16), bblk_ref[...])
    a_re = abar_ref[0:1, :]
    a_im = abar_ref[1:2, :]

    def advance(h_re, h_im, x_re, x_im):
        return a_re * h_re - a_im * h_im + x_re, a_re * h_im + a_im * h_re + x_im

    if rows % 8 == 0:
        def step(t, carry):
            r0 = pl.multiple_of(t * rows, rows)
            n_re, n_im = advance(h_ref[:, :nstate], h_ref[:, nstate:],
                                 bu_ref[pl.ds(r0, rows), :nstate], bu_ref[pl.ds(r0, rows), nstate:])
            h_ref[:, :nstate] = n_re
            h_ref[:, nstate:] = n_im
            bu_ref[pl.ds(r0, rows), :nstate] = n_re
            bu_ref[pl.ds(r0, rows), nstate:] = n_im
            return carry

        lax.fori_loop(0, steps, step, 0)
    else:
        assert rows == 4 and steps % 2 == 0
        top = lax.broadcasted_iota(jnp.int32, (8, nstate), 0) < 4

        def step2(t, carry):
            r0 = pl.multiple_of(t * 8, 8)
            x_re = bu_ref[pl.ds(r0, 8), :nstate]
            x_im = bu_ref[pl.ds(r0, 8), nstate:]
            e_re, e_im = advance(pltpu.roll(h_ref[:, :nstate], 4, axis=0),
                                 pltpu.roll(h_ref[:, nstate:], 4, axis=0), x_re, x_im)
            o_re, o_im = advance(pltpu.roll(e_re, 4, axis=0), pltpu.roll(e_im, 4, axis=0), x_re, x_im)
            n_re = jnp.where(top, e_re, o_re)
            n_im = jnp.where(top, e_im, o_im)
            h_ref[:, :nstate] = n_re
            h_ref[:, nstate:] = n_im
            bu_ref[pl.ds(r0, 8), :nstate] = n_re
            bu_ref[pl.ds(r0, 8), nstate:] = n_im
            return carry

        lax.fori_loop(0, steps // 2, step2, 0)
    hlast_ref[...] = h_ref[...]
    y = (_mm(bu_ref[:, :nstate].astype(BF16), cre_ref[...])
         - _mm(bu_ref[:, nstate:].astype(BF16), cim_ref[...]) + d_ref[...] * u)
    y = jax.nn.gelu(y)
    o_ref[...] = y * jax.nn.sigmoid(_mm(y.astype(BF16), wglu_ref[...]) + bglu_ref[...])


def _ssm_scan(u, h0, abar, bblk, cre, cim, d, wglu, bglu, rows, steps_per_chunk):
    total, wc = u.shape
    nstate = abar.shape[1]
    chunk = rows * steps_per_chunk
    assert total % chunk == 0
    srows = max(rows, 8)
    h0 = jnp.pad(h0, ((srows - rows, 0), (0, 0)))
    const = lambda shape: pl.BlockSpec(shape, lambda c: (0,) * len(shape))
    kern = functools.partial(_ssm_scan_kernel, rows=rows, steps=steps_per_chunk, nstate=nstate)
    y, h_last = pl.pallas_call(
        kern,
        grid=(total // chunk,),
        in_specs=[pl.BlockSpec((chunk, wc), lambda c: (c, 0)),
                  const((srows, 2 * nstate)), const((2, nstate)), const((wc, 2 * nstate)),
                  const((nstate, wc)), const((nstate, wc)), const((1, wc)), const((wc, wc)),
                  const((1, wc))],
        out_specs=[pl.BlockSpec((chunk, wc), lambda c: (c, 0)), const((srows, 2 * nstate))],
        out_shape=[jax.ShapeDtypeStruct((total, wc), F32),
                   jax.ShapeDtypeStruct((srows, 2 * nstate), F32)],
        scratch_shapes=[pltpu.VMEM((srows, 2 * nstate), F32), pltpu.VMEM((chunk, 2 * nstate), F32)],
        compiler_params=_params(("arbitrary",), 48),
        name="ssm_scan",
    )(u, h0, abar, bblk, cre, cim, d, wglu, bglu)
    return y, h_last[srows - rows:]


def _post_kernel(x_ref, osb_ref, omb_ref, oc_ref, gsb_ref, gmb_ref, gc_ref,
                 gl0, gl1, gl2, gl3, gl4, gl5, bg_ref, wb_ref, wo_ref, lng_ref, lnb_ref,
                 y_ref, yb_ref, *, w_sb, w_mb, alpha):
    gls = ((gl0, gl1), (gl2, gl3), (gl4, gl5))
    acts = (osb_ref[...] * jax.nn.silu(gsb_ref[...]),
            omb_ref[...] * jax.nn.silu(gmb_ref[...]),
            oc_ref[...] * jax.nn.silu(gc_ref[...]))
    cuts = (0, w_sb, w_sb + w_mb, wb_ref.shape[0])
    merged = None
    for i in range(3):
        proj = _mm(acts[i].astype(BF16), wb_ref[cuts[i]:cuts[i + 1], :])
        logits = jnp.concatenate([gls[i][0][...], gls[i][1][...]], axis=1) + bg_ref[i:i + 1, :]
        term = jax.nn.sigmoid(logits) * proj
        merged = term if merged is None else merged + term
    yv = alpha * x_ref[...] + _mm(merged.astype(BF16), wo_ref[...])
    mu = jnp.mean(yv, axis=1, keepdims=True)
    cen = yv - mu
    var = jnp.mean(cen * cen, axis=1, keepdims=True)
    out = cen * lax.rsqrt(var + LN_EPS) * lng_ref[...] + lnb_ref[...]
    y_ref[...] = out
    yb_ref[...] = out.astype(BF16)


def _post(x, p, o_sb, o_mb, o_c, b_gate, wb, wo, ln_g, ln_b, cols, alpha, tm=256):
    m, dm = x.shape
    w_sb, w_mb, w_c = o_sb.shape[1], o_mb.shape[1], o_c.shape[1]
    half = dm // 2
    assert m % tm == 0 and cols["g_sb"] % w_sb == 0 and cols["g_mb"] % w_mb == 0
    assert cols["g_c"] % w_c == 0 and cols["gate"] % half == 0
    row = lambda width, blk: pl.BlockSpec((tm, width), lambda i: (i, blk))
    const = lambda shape: pl.BlockSpec(shape, lambda i: (0,) * len(shape), pipeline_mode=pl.Buffered(1))
    gl_specs = [row(half, cols["gate"] // half + j) for j in range(6)]
    kern = functools.partial(_post_kernel, w_sb=w_sb, w_mb=w_mb, alpha=alpha)
    return pl.pallas_call(
        kern,
        grid=(m // tm,),
        in_specs=[row(dm, 0), row(w_sb, 0), row(w_mb, 0), row(w_c, 0),
                  row(w_sb, cols["g_sb"] // w_sb), row(w_mb, cols["g_mb"] // w_mb),
                  row(w_c, cols["g_c"] // w_c)] + gl_specs
                 + [const((3, dm)), const(wb.shape), const(wo.shape), const((1, dm)), const((1, dm))],
        out_specs=[row(dm, 0), row(dm, 0)],
        out_shape=[jax.ShapeDtypeStruct((m, dm), F32), jax.ShapeDtypeStruct((m, dm), BF16)],
        compiler_params=_params(("arbitrary",), 56),
        name="post",
    )(x, o_sb, o_mb, o_c, p, p, p, *([p] * 6), b_gate, wb, wo, ln_g, ln_b)


def _kv_rows_kernel(*refs, depth, heads):
    o_ref = refs[2 * depth]
    layer = pl.program_id(0)
    for l in range(depth):
        @pl.when(layer == l)
        def _(l=l):
            for kv in range(2):
                src = refs[2 * l + kv]
                for h in range(heads):
                    o_ref[:, h, kv, :] = src[:, h * HEAD_DIM:(h + 1) * HEAD_DIM]


def _kv_rows(p_layers, k_col, heads, row0, n_rows, tm=256):
    depth = len(p_layers)
    width = heads * HEAD_DIM
    assert k_col % width == 0 and row0 % tm == 0 and n_rows % tm == 0
    nblk = n_rows // tm

    def spec(l, kv):
        def index(layer, i):
            idle = jnp.where(layer < l, 0, nblk - 1)
            return (row0 // tm + jnp.where(layer == l, i, idle), k_col // width + kv)
        return pl.BlockSpec((tm, width), index)

    return pl.pallas_call(
        functools.partial(_kv_rows_kernel, depth=depth, heads=heads),
        grid=(depth, nblk),
        in_specs=[spec(l, kv) for l in range(depth) for kv in range(2)],
        out_specs=pl.BlockSpec((None, tm, heads, 2, HEAD_DIM), lambda layer, i: (layer, i, 0, 0, 0)),
        out_shape=jax.ShapeDtypeStruct((depth, n_rows, heads, 2, HEAD_DIM), F32),
        compiler_params=_params(("arbitrary", "arbitrary"), 32),
        name="kv_rows",
    )(*[p for p in p_layers for _ in range(2)])


def _block_diag_queries(q, heads):
    n, dec, width = q.shape
    qh = q.reshape(n, dec, heads, HEAD_DIM)
    eye = jnp.eye(heads, dtype=q.dtype)
    out = jnp.einsum("nihd,hg->nhigd", qh, eye)
    return out.reshape(n, heads * dec, width)


def _diag_heads(o, heads, dec):
    n = o.shape[0]
    o5 = o.reshape(n, heads, dec, heads, HEAD_DIM)
    d = jnp.stack([o5[:, h, :, h, :] for h in range(heads)], axis=2)
    return d.reshape(n * dec, heads * HEAD_DIM)


def kernel(x_prompt, x_sample, cache_sb, cache_moba, state_ssm, page_table, w_in, b_gate, w_branch,
           w_out, ln_g, ln_b, ssm_a_re, ssm_a_im, ssm_log_step, ssm_b_re, ssm_b_im, ssm_c_re, ssm_c_im,
           ssm_d, ssm_w_glu, ssm_b_glu):
    batch, seq, dm = x_prompt.shape
    n_dec, dec, _ = x_sample.shape
    depth = w_in.shape[0]
    h_sb = cache_sb.shape[4]
    h_mb = cache_moba.shape[4]
    w_sb, w_mb = h_sb * HEAD_DIM, h_mb * HEAD_DIM
    groups, nst = ssm_a_re.shape[1], ssm_a_re.shape[2]
    w_c = groups * SSM_GROUP
    n_p, n_s = batch * seq, n_dec * dec
    alpha = (2 * depth) ** 0.25

    c_sb, c_mb = 0, 4 * w_sb
    c_u = c_mb + 4 * w_mb
    cols = {"g_sb": c_sb + 3 * w_sb, "g_mb": c_mb + 3 * w_mb, "g_c": c_u + w_c, "gate": c_u + 2 * w_c}
    blk = lambda col: col // HEAD_DIM

    x = jnp.concatenate([x_prompt.reshape(n_p, dm), x_sample.reshape(n_s, dm)], axis=0)
    xb = x.astype(BF16)
    cache_sb_t = cache_sb.transpose(0, 1, 2, 4, 3, 5)
    cache_mb_t = cache_moba.transpose(0, 1, 2, 4, 3, 5)
    slope_rows = jnp.repeat(_alibi_slopes(h_mb), dec)[:, None]
    wb_bf = w_branch.astype(BF16)
    wo_bf = w_out.astype(BF16)
    wglu_bf = ssm_w_glu.astype(BF16)
    eye_g = jnp.eye(groups, dtype=F32)

    outs = {k: [] for k in ("ssm_p", "ssm_s")}
    p_layers = []
    for l in range(depth):
        p = _in_proj(xb, w_in, l)
        p_layers.append(p)
        kv_sb = p[n_p:, c_sb + w_sb:c_sb + 3 * w_sb]
        kv_mb = p[n_p:, c_mb + w_mb:c_mb + 3 * w_mb]

        o_sb_p = _sb_prompt(p, batch, seq, h_sb, blk(c_sb), blk(c_sb + w_sb), blk(c_sb + 2 * w_sb))
        o_mb_p = _moba_prompt(p, batch, seq, h_mb, blk(c_mb), blk(c_mb + w_mb), blk(c_mb + 2 * w_mb))

        pad_new = lambda kv: jnp.pad(kv.reshape(n_dec, dec, -1), ((0, 0), (0, 8 - dec), (0, 0)))
        q_sb_s = p[n_p:, c_sb:c_sb + w_sb].reshape(n_dec, dec, w_sb)
        q_mb_s = p[n_p:, c_mb:c_mb + w_mb].reshape(n_dec, dec, w_mb)
        o_sb_s = _sb_sample(_block_diag_queries(q_sb_s, h_sb), pad_new(kv_sb), cache_sb_t, page_table, l, dec)
        o_mb_s = _moba_sample(_block_diag_queries(q_mb_s, h_mb), pad_new(kv_mb), slope_rows, cache_mb_t,
                              page_table, l, dec)
        o_sb = jnp.concatenate([o_sb_p, _diag_heads(o_sb_s, h_sb, dec)], axis=0)
        o_mb = jnp.concatenate([o_mb_p, _diag_heads(o_mb_s, h_mb, dec)], axis=0)

        abar, bblk = _ssm_discretise(ssm_a_re[l], ssm_a_im[l], ssm_log_step[l], ssm_b_re[l], ssm_b_im[l])
        cblk = lambda cm: (eye_g[:, None, :, None] * cm.transpose(0, 2, 1)[:, :, None, :]
                           ).reshape(groups * nst, w_c).astype(BF16)
        ssm_w = (abar, bblk.astype(BF16), cblk(ssm_c_re[l]), cblk(ssm_c_im[l]), ssm_d[l].reshape(1, w_c),
                 wglu_bf[l], ssm_b_glu[l].reshape(1, w_c))
        u = p[:, c_u:c_u + w_c]
        u_p = u[:n_p].reshape(batch, seq, w_c).transpose(1, 0, 2).reshape(n_p, w_c)
        u_s = u[n_p:].reshape(n_dec, dec, w_c).transpose(1, 0, 2).reshape(n_s, w_c)
        y_p, hl_p = _ssm_scan(u_p, jnp.zeros((batch, 2 * groups * nst), F32), *ssm_w,
                              rows=batch, steps_per_chunk=128)
        st = state_ssm[l].reshape(n_dec, groups * nst, 2)
        h0_s = jnp.concatenate([st[..., 0], st[..., 1]], axis=1)
        y_s, hl_s = _ssm_scan(u_s, h0_s, *ssm_w, rows=n_dec, steps_per_chunk=dec)
        o_c = jnp.concatenate([y_p.reshape(seq, batch, w_c).transpose(1, 0, 2).reshape(n_p, w_c),
                               y_s.reshape(dec, n_dec, w_c).transpose(1, 0, 2).reshape(n_s, w_c)], axis=0)
        unstate = lambda hl: jnp.stack([hl[:, :groups * nst], hl[:, groups * nst:]], axis=-1
                                       ).reshape(hl.shape[0], groups, nst, 2)
        outs["ssm_p"].append(unstate(hl_p))
        outs["ssm_s"].append(unstate(hl_s))

        x, xb = _post(x, p, o_sb, o_mb, o_c, b_gate[l], wb_bf[l], wo_bf[l], ln_g[l].reshape(1, dm),
                      ln_b[l].reshape(1, dm), cols, alpha)

    kv_p = lambda col, heads: _kv_rows(p_layers, col, heads, 0, n_p).transpose(0, 1, 3, 2, 4).reshape(
        depth, batch, seq, 2, heads, HEAD_DIM)
    kv_s = lambda col, heads: _kv_rows(p_layers, col, heads, n_p, n_s).transpose(0, 1, 3, 2, 4).reshape(
        depth, n_dec, dec, 2, heads, HEAD_DIM)
    return (x[:n_p].reshape(batch, seq, dm), x[n_p:].reshape(n_dec, dec, dm),
            kv_p(c_sb + w_sb, h_sb), kv_p(c_mb + w_mb, h_mb), jnp.stack(outs["ssm_p"]),
            kv_s(c_sb + w_sb, h_sb), kv_s(c_mb + w_mb, h_mb), jnp.stack(outs["ssm_s"]))
```
